```python
import jax, jax.numpy as jnp
from jax import lax
import numpy as np

D_MODEL = 1024
BATCH = 8
SEQ = 4096
DEPTH = 4

HEAD_DIM = 64
FOX_HEADS = 8
FOX_WIDTH = FOX_HEADS * HEAD_DIM
POOL_GROUPS = 4
POOL_WINDOWS = (2, 4, 8, 16)
POOL_WIDTH = D_MODEL - FOX_WIDTH
POOL_GROUP_DIM = POOL_WIDTH // POOL_GROUPS
Q_BLOCK = 128
IN_COLS = 4 * FOX_WIDTH + FOX_HEADS + POOL_WIDTH
SPLITS = (FOX_WIDTH, 2 * FOX_WIDTH, 3 * FOX_WIDTH, 4 * FOX_WIDTH, 4 * FOX_WIDTH + FOX_HEADS)
RWKV_HEADS = D_MODEL // HEAD_DIM
DECAY_LORA = 64
AAA_LORA = 64
MV_LORA = 32
GATE_LORA = 160
D_FF = -(-(8 * D_MODEL) // (3 * 256)) * 256
RMS_EPS = 1e-6
GN_EPS = 64e-5
N_EVEN = (DEPTH + 1) // 2
N_ODD = DEPTH // 2

kernel_name = 'fox_pool_rwkv7_hybrid_trunk'


def _rmsnorm(x, gain):
    xf = x.astype(jnp.float32)
    y = xf * lax.rsqrt(jnp.mean(xf * xf, axis=-1, keepdims=True) + RMS_EPS)
    return (y * gain.astype(jnp.float32)).astype(x.dtype)


def _swiglu(h, w_gate, w_up, w_down):
    return (jax.nn.silu(h @ w_gate) * (h @ w_up)) @ w_down


def _token_shift(x):
    return jnp.pad(x, ((0, 0), (1, 0), (0, 0)))[:, :-1]


def _forgetting_attention(q, k, v, cum):
    b, h, s, dh = q.shape
    nb = s // Q_BLOCK
    scale = dh ** -0.5
    qb = q.reshape(b, h, nb, Q_BLOCK, dh).transpose(2, 0, 1, 3, 4)
    cb = cum.reshape(b, h, nb, Q_BLOCK).transpose(2, 0, 1, 3)
    key_pos = jnp.arange(s)

    def one_block(args):
        i, q_i, c_i = args
        q_pos = i * Q_BLOCK + jnp.arange(Q_BLOCK)
        logits = (jnp.einsum('bhqd,bhkd->bhqk', q_i, k).astype(jnp.float32) * scale
                  + c_i[..., :, None] - cum[..., None, :])
        causal = key_pos[None, :] <= q_pos[:, None]
        p = jax.nn.softmax(jnp.where(causal, logits, -1e30), axis=-1)
        return jnp.einsum('bhqk,bhkd->bhqd', p.astype(v.dtype), v)

    out = lax.map(one_block, (jnp.arange(nb), qb, cb))
    return out.transpose(1, 2, 0, 3, 4).reshape(b, h, s, dh)


def _multiscale_causal_pool(u, pool_w, pool_scale):
    b, s, _ = u.shape
    uf = u.astype(jnp.float32).reshape(b, s, POOL_GROUPS, POOL_GROUP_DIM)
    csum = jnp.cumsum(uf, axis=1)
    pos = jnp.arange(s)
    groups = []
    for g, w in enumerate(POOL_WINDOWS):
        cg = csum[:, :, g]
        prev = jnp.pad(cg, ((0, 0), (w, 0), (0, 0)))[:, :s]
        count = jnp.minimum(pos + 1, w).astype(jnp.float32)[None, :, None]
        groups.append((cg - prev) / count - uf[:, :, g])
    pooled = jnp.stack(groups, axis=2).astype(u.dtype)
    mixed = jnp.einsum('bsgc,gcd->bsgd', pooled, pool_w)
    return mixed.reshape(b, s, POOL_WIDTH) * pool_scale


def _fox_pool_mixer(h, w_in, f_bias, q_gain, k_gain, pool_w, pool_scale, w_out):
    b, s, _ = h.shape
    proj = h @ w_in
    q, k, v, og, f_logit, u = jnp.split(proj, SPLITS, axis=-1)

    def heads(t):
        return t.reshape(b, s, FOX_HEADS, HEAD_DIM)

    q = _rmsnorm(heads(q), q_gain).transpose(0, 2, 1, 3)
    k = _rmsnorm(heads(k), k_gain).transpose(0, 2, 1, 3)
    v = heads(v).transpose(0, 2, 1, 3)
    log_f = jax.nn.log_sigmoid(f_logit.astype(jnp.float32) + f_bias.astype(jnp.float32))
    cum = jnp.cumsum(log_f, axis=1).transpose(0, 2, 1)
    attn = _forgetting_attention(q, k, v, cum).transpose(0, 2, 1, 3).reshape(b, s, FOX_WIDTH)
    attn = attn * jax.nn.sigmoid(og)
    pool = _multiscale_causal_pool(u, pool_w, pool_scale)
    return jnp.concatenate([attn, pool], axis=-1) @ w_out


def _rwkv7_step(state, inp):
    r_t, w_t, k_t, v_t, a_t, b_t = inp
    sa = jnp.einsum('bhvk,bhk->bhv', state, a_t)
    state = (state * w_t[:, :, None, :] + sa[..., None] * b_t[:, :, None, :]
             + v_t[..., None] * k_t[:, :, None, :])
    return state, jnp.einsum('bhvk,bhk->bhv', state, r_t)


def _rwkv7_time_mix(h, mu, w_r, w_k, w_v, w0, w1, w2, a0, a1, a2, g1, g2,
                    k_k, k_a, r_k, ln_w, ln_b, w_o, v_first, v_mix):
    b, s, d = h.shape
    f32 = jnp.float32
    xx = _token_shift(h) - h
    xr, xw, xk, xv, xa, xg = [h + xx * mu[i] for i in range(6)]
    r = (xr @ w_r).astype(f32)
    k = (xk @ w_k).astype(f32)
    v = (xv @ w_v).astype(f32)
    w = -jax.nn.softplus(-(w0 + jnp.tanh(xw @ w1) @ w2).astype(f32)) - 0.5
    a = jax.nn.sigmoid((a0 + (xa @ a1) @ a2).astype(f32))
    g = jax.nn.sigmoid(xg @ g1) @ g2
    if v_mix is None:
        v_first = v
    else:
        v0, v1, v2 = v_mix
        v = v + (v_first - v) * jax.nn.sigmoid((v0 + (xv @ v1) @ v2).astype(f32))

    def heads(t):
        return t.reshape(b, s, RWKV_HEADS, HEAD_DIM)

    kk = heads(k * k_k.astype(f32))
    kk = kk / jnp.maximum(jnp.sqrt(jnp.sum(kk * kk, axis=-1, keepdims=True)), 1e-12)
    k = k * (1.0 + (a - 1.0) * k_a.astype(f32))
    decay = jnp.exp(-jnp.exp(w))
    rh, kh, vh, ah = heads(r), heads(k), heads(v), heads(a)
    seq_first = lambda t: t.transpose(1, 0, 2, 3)
    xs = (seq_first(rh), seq_first(heads(decay)), seq_first(kh), seq_first(vh),
          seq_first(-kk), seq_first(kk * ah))
    state0 = jnp.zeros((b, RWKV_HEADS, HEAD_DIM, HEAD_DIM), f32)
    _, ys = lax.scan(_rwkv7_step, state0, xs)
    y = ys.transpose(1, 0, 2, 3)
    mean = jnp.mean(y, axis=-1, keepdims=True)
    var = jnp.mean(jnp.square(y - mean), axis=-1, keepdims=True)
    y = ((y - mean) * lax.rsqrt(var + GN_EPS)).reshape(b, s, d) * ln_w.astype(f32) + ln_b.astype(f32)
    bonus = jnp.sum(rh * kh * r_k.astype(f32), axis=-1, keepdims=True) * vh
    y = (y + bonus.reshape(b, s, d)).astype(h.dtype)
    return (y * g) @ w_o, v_first


def setup_inputs(seed: int = 0) -> dict:
    key = jax.random.key(seed)
    keys = jax.random.split(key, 40)
    counter = [0]
    f32 = jnp.float32

    def nk():
        counter[0] += 1
        return keys[counter[0] - 1]

    def nrm(shape, scale):
        return jax.random.normal(nk(), shape, f32) * scale

    def gain(shape):
        return 1.0 + 0.1 * jax.random.normal(nk(), shape, f32)

    D, F, ne, no = D_MODEL, D_FF, N_EVEN, N_ODD
    return {
        'x': nrm((BATCH, SEQ, D), 1.0),
        'mix_norm': gain((DEPTH, D)),
        'ffn_norm': gain((DEPTH, D)),
        'ffn_w_gate': nrm((DEPTH, D, F), D ** -0.5),
        'ffn_w_up': nrm((DEPTH, D, F), D ** -0.5),
        'ffn_w_down': nrm((DEPTH, F, D), F ** -0.5),
        'hy_w_in': nrm((ne, D, IN_COLS), D ** -0.5),
        'hy_f_bias': 2.0 + 0.5 * jax.random.normal(nk(), (ne, FOX_HEADS), f32),
        'hy_q_gain': gain((ne, HEAD_DIM)),
        'hy_k_gain': gain((ne, HEAD_DIM)),
        'hy_pool_w': nrm((ne, POOL_GROUPS, POOL_GROUP_DIM, POOL_GROUP_DIM), POOL_GROUP_DIM ** -0.5),
        'hy_pool_scale': gain((ne, POOL_WIDTH)),
        'hy_w_out': nrm((ne, D, D), D ** -0.5),
        'rw_mu': jax.random.uniform(nk(), (no, 6, D), f32),
        'rw_w_r': nrm((no, D, D), D ** -0.5),
        'rw_w_k': nrm((no, D, D), D ** -0.5),
        'rw_w_v': nrm((no, D, D), D ** -0.5),
        'rw_w0': nrm((no, D), 0.5),
        'rw_w1': nrm((no, D, DECAY_LORA), D ** -0.5),
        'rw_w2': nrm((no, DECAY_LORA, D), 0.5 * DECAY_LORA ** -0.5),
        'rw_a0': nrm((no, D), 0.5),
        'rw_a1': nrm((no, D, AAA_LORA), D ** -0.5),
        'rw_a2': nrm((no, AAA_LORA, D), 0.5 * AAA_LORA ** -0.5),
        'rw_g1': nrm((no, D, GATE_LORA), D ** -0.5),
        'rw_g2': nrm((no, GATE_LORA, D), GATE_LORA ** -0.5),
        'rw_k_k': gain((no, D)),
        'rw_k_a': gain((no, D)),
        'rw_r_k': nrm((no, RWKV_HEADS, HEAD_DIM), 0.1),
        'rw_ln_w': gain((no, D)),
        'rw_ln_b': nrm((no, D), 0.02),
        'rw_w_o': nrm((no, D, D), D ** -0.5),
        'rw_v0': nrm((max(no - 1, 0), D), 0.5),
        'rw_v1': nrm((max(no - 1, 0), D, MV_LORA), D ** -0.5),
        'rw_v2': nrm((max(no - 1, 0), MV_LORA, D), 0.5 * MV_LORA ** -0.5),
    }


def reference(x, mix_norm, ffn_norm, ffn_w_gate, ffn_w_up, ffn_w_down,
              hy_w_in, hy_f_bias, hy_q_gain, hy_k_gain, hy_pool_w, hy_pool_scale, hy_w_out,
              rw_mu, rw_w_r, rw_w_k, rw_w_v, rw_w0, rw_w1, rw_w2, rw_a0, rw_a1, rw_a2,
              rw_g1, rw_g2, rw_k_k, rw_k_a, rw_r_k, rw_ln_w, rw_ln_b, rw_w_o,
              rw_v0, rw_v1, rw_v2):
    v_first = None
    for layer in range(DEPTH):
        h = _rmsnorm(x, mix_norm[layer])
        if layer % 2 == 0:
            e = layer // 2
            y = _fox_pool_mixer(h, hy_w_in[e], hy_f_bias[e], hy_q_gain[e], hy_k_gain[e],
                                hy_pool_w[e], hy_pool_scale[e], hy_w_out[e])
        else:
            o = layer // 2
            v_mix = None if o == 0 else (rw_v0[o - 1], rw_v1[o - 1], rw_v2[o - 1])
            y, v_first = _rwkv7_time_mix(h, rw_mu[o], rw_w_r[o], rw_w_k[o], rw_w_v[o],
                                         rw_w0[o], rw_w1[o], rw_w2[o], rw_a0[o], rw_a1[o], rw_a2[o],
                                         rw_g1[o], rw_g2[o], rw_k_k[o], rw_k_a[o], rw_r_k[o],
                                         rw_ln_w[o], rw_ln_b[o], rw_w_o[o], v_first, v_mix)
        x = x + y
        x = x + _swiglu(_rmsnorm(x, ffn_norm[layer]), ffn_w_gate[layer], ffn_w_up[layer], ffn_w_down[layer])
    return x
```

```python
import functools

import jax
import jax.numpy as jnp
from jax import lax
from jax.experimental import pallas as pl
from jax.experimental.pallas import tpu as pltpu

F32 = jnp.float32
BF16 = jnp.bfloat16

D_MODEL = 1024
HEAD_DIM = 64
LANES = 128
FOX_HEADS = 8
FOX_WIDTH = FOX_HEADS * HEAD_DIM
FOX_PAIRS = FOX_HEADS // 2
POOL_WINDOWS = (2, 4, 8, 16)
POOL_WIDTH = D_MODEL - FOX_WIDTH
POOL_HALO = 16
RMS_EPS = 1e-6
GN_EPS = 64e-5
CHUNK = 64
DECAY_SCALE = 0.6065306597126334
NEG_BIG = -1e30
VMEM_LIMIT = 56 * 1024 * 1024


def _cparams(*sem):
    return pltpu.CompilerParams(dimension_semantics=sem, vmem_limit_bytes=VMEM_LIMIT)


def _rms(x, gain):
    ms = jnp.mean(x * x, axis=-1, keepdims=True)
    return x * lax.rsqrt(ms + RMS_EPS) * gain


def _dot(a, b):
    return jnp.dot(a, b, preferred_element_type=F32)


def _dot_nt(a, b):
    return lax.dot_general(a, b, (((1,), (1,)), ((), ())), preferred_element_type=F32)


def _dot_tn(a, b):
    return lax.dot_general(a, b, (((0,), (0,)), ((), ())), preferred_element_type=F32)


def _const_spec(shape):
    nd = len(shape)
    return pl.BlockSpec(shape, lambda *_: (0,) * nd)


def _inproj_even_kernel(x_ref, gain_ref, w_ref, wf_ref, fb_ref,
                        qkvg_ref, u_ref, cum_ref, carry_ref, *, tiles_per_seq):
    i = pl.program_id(0)
    hb = _rms(x_ref[...], gain_ref[...]).astype(BF16)
    main = _dot(hb, w_ref[...])
    qkvg_ref[...] = main[:, :4 * FOX_WIDTH].astype(BF16)
    u_ref[...] = main[:, 4 * FOX_WIDTH:]
    ft = _dot_nt(wf_ref[...], hb)[:FOX_HEADS]
    z = ft + fb_ref[...]
    c = jnp.minimum(z, 0.0) - jnp.log1p(jnp.exp(-jnp.abs(z)))
    tm = c.shape[1]
    lane = lax.broadcasted_iota(jnp.int32, c.shape, 1)
    sh = 1
    while sh < tm:
        c = c + jnp.where(lane >= sh, pltpu.roll(c, sh, axis=1), 0.0)
        sh *= 2

    @pl.when(i % tiles_per_seq == 0)
    def _():
        carry_ref[...] = jnp.zeros_like(carry_ref)

    c = c + carry_ref[...]
    cum_ref[...] = c
    carry_ref[...] = c[:, tm - 1:tm]


def _inproj_even(x, gain, w_main, wf_t, f_bias, seq, tm=512):
    n = x.shape[0]
    tm = min(tm, seq)
    return pl.pallas_call(
        functools.partial(_inproj_even_kernel, tiles_per_seq=seq // tm),
        grid=(n // tm,),
        in_specs=[pl.BlockSpec((tm, D_MODEL), lambda i: (i, 0)),
                  _const_spec(gain.shape), _const_spec(w_main.shape),
                  _const_spec(wf_t.shape), _const_spec(f_bias.shape)],
        out_specs=[pl.BlockSpec((tm, 4 * FOX_WIDTH), lambda i: (i, 0)),
                   pl.BlockSpec((tm, POOL_WIDTH), lambda i: (i, 0)),
                   pl.BlockSpec((FOX_HEADS, tm), lambda i: (0, i))],
        out_shape=[jax.ShapeDtypeStruct((n, 4 * FOX_WIDTH), BF16),
                   jax.ShapeDtypeStruct((n, POOL_WIDTH), F32),
                   jax.ShapeDtypeStruct((FOX_HEADS, n), F32)],
        scratch_shapes=[pltpu.VMEM((FOX_HEADS, 1), F32)],
        compiler_params=_cparams("arbitrary"),
    )(x, gain, w_main, wf_t, f_bias)


def _pair_rms_scale(t, first):
    t2 = t * t
    s0 = jnp.sum(jnp.where(first, t2, 0.0), axis=-1, keepdims=True)
    s1 = jnp.sum(jnp.where(first, 0.0, t2), axis=-1, keepdims=True)
    return jnp.where(first, lax.rsqrt(s0 / HEAD_DIM + RMS_EPS), lax.rsqrt(s1 / HEAD_DIM + RMS_EPS))


def _fox_attn_kernel(q_ref, k_ref, v_ref, og_ref, cum_ref, qg_ref, kg_ref, o_ref, kn_ref, *, tq, seq):
    qi = pl.program_id(2)
    first = lax.broadcasted_iota(jnp.int32, (1, LANES), 1) < HEAD_DIM

    @pl.when(qi == 0)
    def _():
        def norm_rows(t, carry):
            off = pl.multiple_of(t * tq, tq)
            k = k_ref[pl.ds(off, tq), :].astype(F32)
            kn_ref[pl.ds(off, tq), :] = (k * _pair_rms_scale(k, first) * kg_ref[...]).astype(BF16)
            return carry
        lax.fori_loop(0, seq // tq, norm_rows, 0)

    q = q_ref[...].astype(F32)
    qn = q * _pair_rms_scale(q, first) * (qg_ref[...] * HEAD_DIM ** -0.5)
    q_heads = (jnp.where(first, qn, 0.0).astype(BF16), jnp.where(first, 0.0, qn).astype(BF16))
    c_q = cum_ref[:, pl.ds(pl.multiple_of(qi * tq, tq), tq)]
    c_ref = (c_q[0:1, 0:1], c_q[1:2, 0:1])
    row = lax.broadcasted_iota(jnp.int32, (tq, tq), 0)
    col = lax.broadcasted_iota(jnp.int32, (tq, tq), 1)
    causal = col <= row

    def step(j, carry, diagonal):
        off = pl.multiple_of(j * tq, tq)
        kb = kn_ref[pl.ds(off, tq), :]
        vb = v_ref[pl.ds(off, tq), :]
        cb = cum_ref[:, pl.ds(off, tq)]
        new = []
        for h in range(2):
            m_old, l_old, acc = carry[h]
            s = _dot_nt(q_heads[h], kb) + (c_ref[h] - cb[h:h + 1, :])
            if diagonal:
                s = jnp.where(causal, s, NEG_BIG)
            m_new = jnp.maximum(m_old, jnp.max(s, axis=-1, keepdims=True))
            alpha = jnp.exp(m_old - m_new)
            p = jnp.exp(s - m_new)
            l_new = alpha * l_old + jnp.sum(p, axis=-1, keepdims=True)
            acc = alpha * acc + _dot(p.astype(BF16), vb)
            new.append((m_new, l_new, acc))
        return tuple(new)

    init = tuple((jnp.full((tq, 1), NEG_BIG, F32), jnp.zeros((tq, 1), F32), jnp.zeros((tq, LANES), F32))
                 for _ in range(2))
    carry = lax.fori_loop(0, qi, lambda j, c: step(j, c, False), init)
    (_, l0, a0), (_, l1, a1) = step(qi, carry, True)
    attn = jnp.where(first, a0 / l0, a1 / l1)
    o_ref[...] = (attn * jax.nn.sigmoid(og_ref[...].astype(F32))).astype(BF16)


def _fox_attn(qkvg, cum, q_gain2, k_gain2, batch, seq, tq=256):
    n = qkvg.shape[0]
    tq = min(tq, seq)
    nq = seq // tq
    cum_pairs = cum.reshape(FOX_PAIRS, 2, n)
    return pl.pallas_call(
        functools.partial(_fox_attn_kernel, tq=tq, seq=seq),
        grid=(batch, FOX_PAIRS, nq),
        in_specs=[pl.BlockSpec((tq, LANES), lambda b, p, i: (b * nq + i, p)),
                  pl.BlockSpec((seq, LANES), lambda b, p, i: (b, FOX_PAIRS + p)),
                  pl.BlockSpec((seq, LANES), lambda b, p, i: (b, 2 * FOX_PAIRS + p)),
                  pl.BlockSpec((tq, LANES), lambda b, p, i: (b * nq + i, 3 * FOX_PAIRS + p)),
                  pl.BlockSpec((None, 2, seq), lambda b, p, i: (p, 0, b)),
                  _const_spec(q_gain2.shape), _const_spec(k_gain2.shape)],
        out_specs=pl.BlockSpec((tq, LANES), lambda b, p, i: (b * nq + i, p)),
        out_shape=jax.ShapeDtypeStruct((n, FOX_WIDTH), BF16),
        scratch_shapes=[pltpu.VMEM((seq, LANES), BF16)],
        compiler_params=_cparams("parallel", "parallel", "arbitrary"),
    )(qkvg, qkvg, qkvg, qkvg, cum_pairs, q_gain2, k_gain2)


def _split_bf16(t):
    hi = t.astype(BF16)
    lo = (t - hi.astype(F32)).astype(BF16)
    return hi, lo


def _even_post_kernel(x_ref, attn_ref, u_ref, halo_ref, pw_ref, ps_ref, wo_ref, o_ref, *, tiles_per_seq):
    i = pl.program_id(0)
    tm = x_ref.shape[0]
    tile_in_seq = i % tiles_per_seq
    u = u_ref[...]
    halo = halo_ref[...] * (tile_in_seq != 0).astype(F32)
    row = lax.broadcasted_iota(jnp.int32, (tm, tm), 0)
    col = lax.broadcasted_iota(jnp.int32, (tm, tm), 1)
    hrow = lax.broadcasted_iota(jnp.int32, (tm, POOL_HALO), 0)
    hcol = lax.broadcasted_iota(jnp.int32, (tm, POOL_HALO), 1)
    pos = tile_in_seq * tm + lax.broadcasted_iota(jnp.int32, (tm, 1), 0)
    acc = x_ref[...] + _dot(attn_ref[...], wo_ref[0:FOX_WIDTH, :])
    for g, w in enumerate(POOL_WINDOWS):
        sl = slice(g * LANES, (g + 1) * LANES)
        band = jnp.where((col <= row) & (row - col < w), 1.0, 0.0).astype(BF16)
        hband = jnp.where(hrow + POOL_HALO - hcol < w, 1.0, 0.0).astype(BF16)
        u_hi, u_lo = _split_bf16(u[:, sl])
        h_hi, h_lo = _split_bf16(halo[:, sl])
        wsum = _dot(band, u_hi) + _dot(band, u_lo) + _dot(hband, h_hi) + _dot(hband, h_lo)
        count = jnp.minimum(pos + 1, w).astype(F32)
        pooled = wsum / count - u[:, sl]
        mixed = _dot(pooled.astype(BF16), pw_ref[g]) * ps_ref[:, sl]
        acc = acc + _dot(mixed.astype(BF16), wo_ref[FOX_WIDTH + g * LANES:FOX_WIDTH + (g + 1) * LANES, :])
    o_ref[...] = acc


def _even_post(x, attn, u, pool_w, pool_scale, w_out, seq, tm=256):
    n = x.shape[0]
    tm = min(tm, seq)
    hb = tm // POOL_HALO
    return pl.pallas_call(
        functools.partial(_even_post_kernel, tiles_per_seq=seq // tm),
        grid=(n // tm,),
        in_specs=[pl.BlockSpec((tm, D_MODEL), lambda i: (i, 0)),
                  pl.BlockSpec((tm, FOX_WIDTH), lambda i: (i, 0)),
                  pl.BlockSpec((tm, POOL_WIDTH), lambda i: (i, 0)),
                  pl.BlockSpec((POOL_HALO, POOL_WIDTH), lambda i: (jnp.maximum(i * hb - 1, 0), 0)),
                  _const_spec(pool_w.shape), _const_spec(pool_scale.shape), _const_spec(w_out.shape)],
        out_specs=pl.BlockSpec((tm, D_MODEL), lambda i: (i, 0)),
        out_shape=jax.ShapeDtypeStruct((n, D_MODEL), F32),
        compiler_params=_cparams("parallel"),
    )(x, attn, u, u, pool_w, pool_scale, w_out)


def _ffn_kernel(x_ref, gain_ref, wg_ref, wu_ref, wd_ref, o_ref, hb_ref):
    j = pl.program_id(1)

    @pl.when(j == 0)
    def _():
        x = x_ref[...]
        hb_ref[...] = _rms(x, gain_ref[...]).astype(BF16)
        o_ref[...] = x

    hb = hb_ref[...]
    gate = _dot(hb, wg_ref[...])
    up = _dot(hb, wu_ref[...])
    act = (gate * jax.nn.sigmoid(gate) * up).astype(BF16)
    o_ref[...] += _dot(act, wd_ref[...])


def _ffn(x, gain, w_gate, w_up, w_down, tm=512, tf=1408):
    n = x.shape[0]
    d_ff = w_gate.shape[1]
    tm = min(tm, n)
    return pl.pallas_call(
        _ffn_kernel,
        grid=(n // tm, d_ff // tf),
        in_specs=[pl.BlockSpec((tm, D_MODEL), lambda i, j: (i, 0)),
                  _const_spec(gain.shape),
                  pl.BlockSpec((D_MODEL, tf), lambda i, j: (0, j)),
                  pl.BlockSpec((D_MODEL, tf), lambda i, j: (0, j)),
                  pl.BlockSpec((tf, D_MODEL), lambda i, j: (j, 0))],
        out_specs=pl.BlockSpec((tm, D_MODEL), lambda i, j: (i, 0)),
        out_shape=jax.ShapeDtypeStruct((n, D_MODEL), F32),
        scratch_shapes=[pltpu.VMEM((tm, D_MODEL), BF16)],
        compiler_params=_cparams("parallel", "arbitrary"),
    )(x, gain, w_gate, w_up, w_down)


def _rwkv_proj_kernel(*refs, tiles_per_seq, has_vmix):
    if has_vmix:
        (x_ref, halo_ref, gain_ref, mu_ref, wr_ref, wk_ref, wv_ref, w0_ref, w1_ref, w2_ref,
         a0_ref, a1_ref, a2_ref, g1_ref, g2_ref, vf_ref, v0_ref, v1_ref, v2_ref,
         r_ref, lw_ref, k_ref, v_ref, a_ref, g_ref) = refs
    else:
        (x_ref, halo_ref, gain_ref, mu_ref, wr_ref, wk_ref, wv_ref, w0_ref, w1_ref, w2_ref,
         a0_ref, a1_ref, a2_ref, g1_ref, g2_ref,
         r_ref, lw_ref, k_ref, v_ref, a_ref, g_ref) = refs
    i = pl.program_id(0)
    tm = x_ref.shape[0]
    gain = gain_ref[...]
    h = _rms(x_ref[...], gain)
    h_last = _rms(halo_ref[...], gain)[halo_ref.shape[0] - 1:, :]
    h_last = h_last * (i % tiles_per_seq != 0).astype(F32)
    rows = lax.broadcasted_iota(jnp.int32, (tm, 1), 0)
    h_prev = jnp.where(rows == 0, h_last, pltpu.roll(h, 1, axis=0))
    xx = h_prev - h

    def mix(idx):
        return (h + xx * mu_ref[idx:idx + 1, :]).astype(BF16)

    xr, xw, xk, xv, xa, xg = (mix(idx) for idx in range(6))
    r_ref[...] = _dot(xr, wr_ref[...])
    k_ref[...] = _dot(xk, wk_ref[...])
    v = _dot(xv, wv_ref[...])
    wl = w0_ref[...] + _dot(jnp.tanh(_dot(xw, w1_ref[...])).astype(BF16), w2_ref[...])
    lw_ref[...] = -DECAY_SCALE * jax.nn.sigmoid(wl)
    a_ref[...] = jax.nn.sigmoid(a0_ref[...] + _dot(_dot(xa, a1_ref[...]).astype(BF16), a2_ref[...]))
    g_ref[...] = _dot(jax.nn.sigmoid(_dot(xg, g1_ref[...])).astype(BF16), g2_ref[...])
    if has_vmix:
        gate = jax.nn.sigmoid(v0_ref[...] + _dot(_dot(xv, v1_ref[...]).astype(BF16), v2_ref[...]))
        v = v + (vf_ref[...] - v) * gate
    v_ref[...] = v


def _rwkv_proj(x, gain, p, v_first, seq, tm=256):
    n = x.shape[0]
    tm = min(tm, seq)
    has_vmix = v_first is not None
    tile = pl.BlockSpec((tm, D_MODEL), lambda i: (i, 0))
    halo = pl.BlockSpec((8, D_MODEL), lambda i: (jnp.maximum(i * (tm // 8) - 1, 0), 0))
    consts = [gain, p['mu'], p['w_r'], p['w_k'], p['w_v'], p['w0'], p['w1'], p['w2'],
              p['a0'], p['a1'], p['a2'], p['g1'], p['g2']]
    args = [x, x] + consts
    specs = [tile, halo] + [_const_spec(c.shape) for c in consts]
    if has_vmix:
        extra = [p['v0'], p['v1'], p['v2']]
        args += [v_first] + extra
        specs += [tile] + [_const_spec(c.shape) for c in extra]
    out = jax.ShapeDtypeStruct((n, D_MODEL), F32)
    return pl.pallas_call(
        functools.partial(_rwkv_proj_kernel, tiles_per_seq=seq // tm, has_vmix=has_vmix),
        grid=(n // tm,),
        in_specs=specs,
        out_specs=[tile] * 6,
        out_shape=[out] * 6,
        compiler_params=_cparams("parallel"),
    )(*args)


def _rwkv_scan_kernel(r_ref, lw_ref, k_ref, v_ref, a_ref, g_ref, kk_ref, ka_ref, rk_ref, lnw_ref, lnb_ref,
                      o_ref, state_ref):
    t = pl.program_id(2)

    @pl.when(t == 0)
    def _():
        state_ref[...] = jnp.zeros_like(state_ref)

    L = CHUNK
    n2 = 2 * L
    first = lax.broadcasted_iota(jnp.int32, (1, LANES), 1) < HEAD_DIM
    rho = lax.broadcasted_iota(jnp.int32, (n2, n2), 0)
    sig = lax.broadcasted_iota(jnp.int32, (n2, n2), 1)
    same16 = (rho >> 4) == (sig >> 4)
    same32 = (rho >> 5) == (sig >> 5)
    same64 = (rho >> 6) == (sig >> 6)
    strict = same64 & (sig < rho)
    incl = same64 & (sig <= rho)
    eye = jnp.where(rho == sig, 1.0, 0.0)
    tri = jnp.where(lax.broadcasted_iota(jnp.int32, (L, L), 1) <= lax.broadcasted_iota(jnp.int32, (L, L), 0),
                    1.0, 0.0)

    def stack(z):
        return jnp.concatenate([jnp.where(first, z, 0.0), jnp.where(first, 0.0, z)], axis=0)

    def head_sums(z):
        s0 = jnp.sum(jnp.where(first, z, 0.0), axis=-1, keepdims=True)
        s1 = jnp.sum(jnp.where(first, 0.0, z), axis=-1, keepdims=True)
        return jnp.where(first, s0, s1)

    def bdot(x, y):
        return _dot(x.astype(BF16), y.astype(BF16))

    state = state_ref[...]
    for c in range(r_ref.shape[0] // L):
        rows = slice(c * L, (c + 1) * L)
        r, lw, kraw, v, asig = r_ref[rows, :], lw_ref[rows, :], k_ref[rows, :], v_ref[rows, :], a_ref[rows, :]
        kk = kraw * kk_ref[...]
        kk = kk / jnp.maximum(jnp.sqrt(head_sums(kk * kk)), 1e-12)
        k = kraw * (1.0 + (asig - 1.0) * ka_ref[...])
        a = -kk
        b = kk * asig
        cum = jnp.dot(tri, lw, preferred_element_type=F32, precision=lax.Precision.HIGHEST)
        cum_end = cum[L - 1:L, :]
        e_pos = jnp.exp(cum)
        e_neg = jnp.exp(-cum)
        e_end = jnp.exp(cum_end - cum)
        rt_s = stack(r * e_pos).astype(BF16)
        at_s = stack(a * jnp.exp(cum - lw)).astype(BF16)
        bt = (b * e_neg).astype(BF16)
        kt = (k * e_neg).astype(BF16)
        bh_s = stack(b * e_end).astype(BF16)
        kh_s = stack(k * e_end).astype(BF16)
        v_s = stack(v)
        v_sb = v_s.astype(BF16)
        xs = jnp.concatenate([at_s, rt_s], axis=0)
        ys = jnp.concatenate([bt, bt, kt, kt], axis=0)
        aa = _dot_nt(xs, ys)
        a_ab = jnp.where(strict, aa[:n2, :n2], 0.0)
        a_ak = jnp.where(strict, aa[:n2, n2:], 0.0)
        a_rb = jnp.where(incl, aa[n2:, :n2], 0.0)
        a_rk = jnp.where(incl, aa[n2:, n2:], 0.0)
        d16 = jnp.where(same16, a_ab, 0.0)
        tinv = eye + d16
        pw = d16
        for _ in range(3):
            pw = bdot(pw, pw)
            tinv = tinv + bdot(tinv, pw)
        off = jnp.where(same32 & ~same16, a_ab, 0.0)
        tinv = tinv + bdot(bdot(tinv, off), tinv)
        off = jnp.where(same64 & ~same32, a_ab, 0.0)
        tinv = tinv + bdot(bdot(tinv, off), tinv)
        state_b = state.astype(BF16)
        w = _dot_nt(at_s, state_b) + bdot(a_ak, v_sb)
        u = bdot(tinv, w)
        u_b = u.astype(BF16)
        y_s = _dot_nt(rt_s, state_b) + bdot(a_rb, u_b) + bdot(a_rk, v_sb)
        state = state * jnp.exp(cum_end) + _dot_tn(u_b, bh_s) + _dot_tn(v_sb, kh_s)
        mean = jnp.sum(y_s, axis=-1, keepdims=True) / HEAD_DIM
        dev = jnp.where(same64, y_s - mean, 0.0)
        var = jnp.sum(dev * dev, axis=-1, keepdims=True) / HEAD_DIM
        z_s = jnp.where(same64, dev * lax.rsqrt(var + GN_EPS) * lnw_ref[...] + lnb_ref[...], 0.0)
        z = z_s[:L] + z_s[L:]
        bonus = head_sums(r * k * rk_ref[...]) * v
        o_ref[rows, :] = ((z + bonus) * g_ref[rows, :]).astype(BF16)
    state_ref[...] = state


def _rwkv_scan(r, lw, k, v, a, g, p, batch, seq, tb=256):
    n = r.shape[0]
    tb = min(tb, seq)
    nt = seq // tb
    tile = pl.BlockSpec((tb, LANES), lambda b, h, t: (b * nt + t, h))
    lane_const = pl.BlockSpec((1, LANES), lambda b, h, t: (0, h))
    return pl.pallas_call(
        _rwkv_scan_kernel,
        grid=(batch, D_MODEL // LANES, nt),
        in_specs=[tile] * 6 + [lane_const] * 5,
        out_specs=tile,
        out_shape=jax.ShapeDtypeStruct((n, D_MODEL), BF16),
        scratch_shapes=[pltpu.VMEM((LANES, LANES), F32)],
        compiler_params=_cparams("parallel", "parallel", "arbitrary"),
    )(r, lw, k, v, a, g, p['k_k'], p['k_a'], p['r_k'], p['ln_w'], p['ln_b'])


def _proj_residual_kernel(x_ref, y_ref, w_ref, o_ref):
    o_ref[...] = x_ref[...] + _dot(y_ref[...], w_ref[...])


def _proj_residual(x, y, w, tm=512):
    n = x.shape[0]
    tm = min(tm, n)
    tile = pl.BlockSpec((tm, D_MODEL), lambda i: (i, 0))
    return pl.pallas_call(
        _proj_residual_kernel,
        grid=(n // tm,),
        in_specs=[tile, tile, _const_spec(w.shape)],
        out_specs=tile,
        out_shape=jax.ShapeDtypeStruct((n, D_MODEL), F32),
        compiler_params=_cparams("parallel"),
    )(x, y, w)


def _pad_to(t, axis, size):
    pad = [(0, 0)] * t.ndim
    pad[axis] = (0, size - t.shape[axis])
    return jnp.pad(t, pad)


def _row(t):
    return t.reshape(1, -1).astype(F32)


def _pair_tile(gain):
    return jnp.tile(gain.astype(F32), 2).reshape(1, LANES)


def kernel(x, mix_norm, ffn_norm, ffn_w_gate, ffn_w_up, ffn_w_down, hy_w_in, hy_f_bias, hy_q_gain, hy_k_gain, hy_pool_w, hy_pool_scale, hy_w_out, rw_mu, rw_w_r, rw_w_k, rw_w_v, rw_w0, rw_w1, rw_w2, rw_a0, rw_a1, rw_a2, rw_g1, rw_g2, rw_k_k, rw_k_a, rw_r_k, rw_ln_w, rw_ln_b, rw_w_o, rw_v0, rw_v1, rw_v2):
    batch, seq, d_model = x.shape
    assert d_model == D_MODEL and seq % 256 == 0
    depth = mix_norm.shape[0]
    n = batch * seq
    xs = x.reshape(n, d_model)
    v_first = None
    for layer in range(depth):
        gain = _row(mix_norm[layer])
        if layer % 2 == 0:
            e = layer // 2
            w_in = hy_w_in[e]
            w_main = jnp.concatenate([w_in[:, :4 * FOX_WIDTH], w_in[:, 4 * FOX_WIDTH + FOX_HEADS:]], axis=1).astype(BF16)
            wf_t = _pad_to(w_in[:, 4 * FOX_WIDTH:4 * FOX_WIDTH + FOX_HEADS].T, 0, 16).astype(BF16)
            f_bias = hy_f_bias[e].reshape(FOX_HEADS, 1).astype(F32)
            qkvg, u, cum = _inproj_even(xs, gain, w_main, wf_t, f_bias, seq)
            attn = _fox_attn(qkvg, cum, _pair_tile(hy_q_gain[e]), _pair_tile(hy_k_gain[e]), batch, seq)
            xs = _even_post(xs, attn, u, hy_pool_w[e].astype(BF16), _row(hy_pool_scale[e]),
                            hy_w_out[e].astype(BF16), seq)
        else:
            o = layer // 2
            p = dict(mu=rw_mu[o].astype(F32),
                     w_r=rw_w_r[o].astype(BF16), w_k=rw_w_k[o].astype(BF16), w_v=rw_w_v[o].astype(BF16),
                     w0=_row(rw_w0[o]), w1=_pad_to(rw_w1[o], 1, LANES).astype(BF16),
                     w2=_pad_to(rw_w2[o], 0, LANES).astype(BF16),
                     a0=_row(rw_a0[o]), a1=_pad_to(rw_a1[o], 1, LANES).astype(BF16),
                     a2=_pad_to(rw_a2[o], 0, LANES).astype(BF16),
                     g1=_pad_to(rw_g1[o], 1, 2 * LANES).astype(BF16),
                     g2=_pad_to(rw_g2[o], 0, 2 * LANES).astype(BF16),
                     k_k=_row(rw_k_k[o]), k_a=_row(rw_k_a[o]), r_k=_row(rw_r_k[o]),
                     ln_w=_row(rw_ln_w[o]), ln_b=_row(rw_ln_b[o]))
            if o > 0:
                p.update(v0=_row(rw_v0[o - 1]), v1=_pad_to(rw_v1[o - 1], 1, LANES).astype(BF16),
                         v2=_pad_to(rw_v2[o - 1], 0, LANES).astype(BF16))
            r, lw, k, v, a, g = _rwkv_proj(xs, gain, p, v_first if o > 0 else None, seq)
            if o == 0:
                v_first = v
            y = _rwkv_scan(r, lw, k, v, a, g, p, batch, seq)
            xs = _proj_residual(xs, y, rw_w_o[o].astype(BF16))
        xs = _ffn(xs, _row(ffn_norm[layer]), ffn_w_gate[layer].astype(BF16), ffn_w_up[layer].astype(BF16),
                  ffn_w_down[layer].astype(BF16))
    return xs.reshape(batch, seq, d_model)
```

```python
import functools

import jax
import jax.numpy as jnp
from jax import lax
from jax.experimental import pallas as pl
from jax.experimental.pallas import tpu as pltpu

F32 = jnp.float32
BF16 = jnp.bfloat16

D_MODEL = 1024
HEAD_DIM = 64
LANES = 128
FOX_HEADS = 8
FOX_WIDTH = FOX_HEADS * HEAD_DIM
FOX_PAIRS = FOX_HEADS // 2
POOL_WINDOWS = (2, 4, 8, 16)
POOL_WIDTH = D_MODEL - FOX_WIDTH
POOL_HALO = 16
RMS_EPS = 1e-6
GN_EPS = 64e-5
CHUNK = 64
DECAY_SCALE = 0.6065306597126334
NEG_BIG = -1e30
LOG2E = 1.4426950408889634
VMEM_LIMIT = 56 * 1024 * 1024


def _cparams(*sem):
    return pltpu.CompilerParams(dimension_semantics=sem, vmem_limit_bytes=VMEM_LIMIT)


def _rms(x, gain):
    ms = jnp.mean(x * x, axis=-1, keepdims=True)
    return x * lax.rsqrt(ms + RMS_EPS) * gain


def _dot(a, b):
    return jnp.dot(a, b, preferred_element_type=F32)


def _dot_nt(a, b):
    return lax.dot_general(a, b, (((1,), (1,)), ((), ())), preferred_element_type=F32)


def _dot_tn(a, b):
    return lax.dot_general(a, b, (((0,), (0,)), ((), ())), preferred_element_type=F32)


def _const_spec(shape):
    nd = len(shape)
    return pl.BlockSpec(shape, lambda *_: (0,) * nd)


def _inproj_even_kernel(x_ref, gain_ref, w_ref, wf_ref, fb_ref,
                        qkvg_ref, u_ref, cum_ref, carry_ref, *, tiles_per_seq):
    i = pl.program_id(0)
    hb = _rms(x_ref[...], gain_ref[...]).astype(BF16)
    main = _dot(hb, w_ref[...])
    qkvg_ref[...] = main[:, :4 * FOX_WIDTH].astype(BF16)
    u_ref[...] = main[:, 4 * FOX_WIDTH:]
    ft = _dot_nt(wf_ref[...], hb)[:FOX_HEADS]
    z = ft + fb_ref[...]
    c = jnp.minimum(z, 0.0) - jnp.log1p(jnp.exp(-jnp.abs(z)))
    tm = c.shape[1]
    lane = lax.broadcasted_iota(jnp.int32, c.shape, 1)
    sh = 1
    while sh < tm:
        c = c + jnp.where(lane >= sh, pltpu.roll(c, sh, axis=1), 0.0)
        sh *= 2

    @pl.when(i % tiles_per_seq == 0)
    def _():
        carry_ref[...] = jnp.zeros_like(carry_ref)

    c = c + carry_ref[...]
    cum_ref[...] = c
    carry_ref[...] = c[:, tm - 1:tm]


def _inproj_even(x, gain, w_main, wf_t, f_bias, seq, tm=512):
    n = x.shape[0]
    tm = min(tm, seq)
    return pl.pallas_call(
        functools.partial(_inproj_even_kernel, tiles_per_seq=seq // tm),
        grid=(n // tm,),
        in_specs=[pl.BlockSpec((tm, D_MODEL), lambda i: (i, 0)),
                  _const_spec(gain.shape), _const_spec(w_main.shape),
                  _const_spec(wf_t.shape), _const_spec(f_bias.shape)],
        out_specs=[pl.BlockSpec((tm, 4 * FOX_WIDTH), lambda i: (i, 0)),
                   pl.BlockSpec((tm, POOL_WIDTH), lambda i: (i, 0)),
                   pl.BlockSpec((FOX_HEADS, tm), lambda i: (0, i))],
        out_shape=[jax.ShapeDtypeStruct((n, 4 * FOX_WIDTH), BF16),
                   jax.ShapeDtypeStruct((n, POOL_WIDTH), F32),
                   jax.ShapeDtypeStruct((FOX_HEADS, n), F32)],
        scratch_shapes=[pltpu.VMEM((FOX_HEADS, 1), F32)],
        compiler_params=_cparams("arbitrary"),
    )(x, gain, w_main, wf_t, f_bias)


def _pair_rms_scale(t, first):
    t2 = t * t
    s0 = jnp.sum(jnp.where(first, t2, 0.0), axis=-1, keepdims=True)
    s1 = jnp.sum(jnp.where(first, 0.0, t2), axis=-1, keepdims=True)
    return jnp.where(first, lax.rsqrt(s0 / HEAD_DIM + RMS_EPS), lax.rsqrt(s1 / HEAD_DIM + RMS_EPS))


def _fox_attn_kernel(q_ref, k_ref, v_ref, og_ref, cum_ref, qg_ref, kg_ref, o_ref, kn_ref, *, tq, seq):
    qi = pl.program_id(2)
    first = lax.broadcasted_iota(jnp.int32, (1, LANES), 1) < HEAD_DIM

    @pl.when(qi == 0)
    def _():
        def norm_rows(t, carry):
            off = pl.multiple_of(t * tq, tq)
            k = k_ref[pl.ds(off, tq), :].astype(F32)
            kn_ref[pl.ds(off, tq), :] = (k * _pair_rms_scale(k, first) * kg_ref[...]).astype(BF16)
            return carry
        lax.fori_loop(0, seq // tq, norm_rows, 0)

    q = q_ref[...].astype(F32)
    qn = q * _pair_rms_scale(q, first) * (qg_ref[...] * (HEAD_DIM ** -0.5 * LOG2E))
    q_heads = (jnp.where(first, qn, 0.0).astype(BF16), jnp.where(first, 0.0, qn).astype(BF16))
    c_q = cum_ref[:, pl.ds(pl.multiple_of(qi * tq, tq), tq)]
    c_ref = (c_q[0:1, 0:1] * LOG2E, c_q[1:2, 0:1] * LOG2E)
    row = lax.broadcasted_iota(jnp.int32, (tq, tq), 0)
    col = lax.broadcasted_iota(jnp.int32, (tq, tq), 1)
    causal = col <= row

    def step(j, carry, diagonal):
        off = pl.multiple_of(j * tq, tq)
        kb = kn_ref[pl.ds(off, tq), :]
        vb = v_ref[pl.ds(off, tq), :]
        cb = cum_ref[:, pl.ds(off, tq)] * LOG2E
        new = []
        for h in range(2):
            m_old, l_old, acc = carry[h]
            s = _dot_nt(q_heads[h], kb) + (c_ref[h] - cb[h:h + 1, :])
            if diagonal:
                s = jnp.where(causal, s, NEG_BIG)
            m_new = jnp.maximum(m_old, jnp.max(s, axis=-1, keepdims=True))
            alpha = jnp.exp2(m_old - m_new)
            p = jnp.exp2(s - m_new)
            l_new = alpha * l_old + jnp.sum(p, axis=-1, keepdims=True)
            acc = alpha * acc + _dot(p.astype(BF16), vb)
            new.append((m_new, l_new, acc))
        return tuple(new)

    init = tuple((jnp.full((tq, 1), NEG_BIG, F32), jnp.zeros((tq, 1), F32), jnp.zeros((tq, LANES), F32))
                 for _ in range(2))
    carry = lax.fori_loop(0, qi, lambda j, c: step(j, c, False), init)
    (_, l0, a0), (_, l1, a1) = step(qi, carry, True)
    attn = jnp.where(first, a0 / l0, a1 / l1)
    o_ref[...] = (attn * jax.nn.sigmoid(og_ref[...].astype(F32))).astype(BF16)


def _fox_attn(qkvg, cum, q_gain2, k_gain2, batch, seq, tq=512):
    n = qkvg.shape[0]
    tq = min(tq, seq)
    nq = seq // tq
    cum_pairs = cum.reshape(FOX_PAIRS, 2, n)
    return pl.pallas_call(
        functools.partial(_fox_attn_kernel, tq=tq, seq=seq),
        grid=(batch, FOX_PAIRS, nq),
        in_specs=[pl.BlockSpec((tq, LANES), lambda b, p, i: (b * nq + i, p)),
                  pl.BlockSpec((seq, LANES), lambda b, p, i: (b, FOX_PAIRS + p)),
                  pl.BlockSpec((seq, LANES), lambda b, p, i: (b, 2 * FOX_PAIRS + p)),
                  pl.BlockSpec((tq, LANES), lambda b, p, i: (b * nq + i, 3 * FOX_PAIRS + p)),
                  pl.BlockSpec((None, 2, seq), lambda b, p, i: (p, 0, b)),
                  _const_spec(q_gain2.shape), _const_spec(k_gain2.shape)],
        out_specs=pl.BlockSpec((tq, LANES), lambda b, p, i: (b * nq + i, p)),
        out_shape=jax.ShapeDtypeStruct((n, FOX_WIDTH), BF16),
        scratch_shapes=[pltpu.VMEM((seq, LANES), BF16)],
        compiler_params=_cparams("parallel", "parallel", "arbitrary"),
    )(qkvg, qkvg, qkvg, qkvg, cum_pairs, q_gain2, k_gain2)


def _split_bf16(t):
    hi = t.astype(BF16)
    lo = (t - hi.astype(F32)).astype(BF16)
    return hi, lo


def _even_post_kernel(x_ref, attn_ref, u_ref, halo_ref, pw_ref, ps_ref, wo_ref, o_ref, *, tiles_per_seq):
    i = pl.program_id(0)
    tm = x_ref.shape[0]
    tile_in_seq = i % tiles_per_seq
    u = u_ref[...]
    halo = halo_ref[...] * (tile_in_seq != 0).astype(F32)
    row = lax.broadcasted_iota(jnp.int32, (tm, tm), 0)
    col = lax.broadcasted_iota(jnp.int32, (tm, tm), 1)
    hrow = lax.broadcasted_iota(jnp.int32, (tm, POOL_HALO), 0)
    hcol = lax.broadcasted_iota(jnp.int32, (tm, POOL_HALO), 1)
    pos = tile_in_seq * tm + lax.broadcasted_iota(jnp.int32, (tm, 1), 0)
    acc = x_ref[...] + _dot(attn_ref[...], wo_ref[0:FOX_WIDTH, :])
    for g, w in enumerate(POOL_WINDOWS):
        sl = slice(g * LANES, (g + 1) * LANES)
        band = jnp.where((col <= row) & (row - col < w), 1.0, 0.0).astype(BF16)
        hband = jnp.where(hrow + POOL_HALO - hcol < w, 1.0, 0.0).astype(BF16)
        u_hi, u_lo = _split_bf16(u[:, sl])
        h_hi, h_lo = _split_bf16(halo[:, sl])
        wsum = _dot(band, u_hi) + _dot(band, u_lo) + _dot(hband, h_hi) + _dot(hband, h_lo)
        count = jnp.minimum(pos + 1, w).astype(F32)
        pooled = wsum / count - u[:, sl]
        mixed = _dot(pooled.astype(BF16), pw_ref[g]) * ps_ref[:, sl]
        acc = acc + _dot(mixed.astype(BF16), wo_ref[FOX_WIDTH + g * LANES:FOX_WIDTH + (g + 1) * LANES, :])
    o_ref[...] = acc


def _even_post(x, attn, u, pool_w, pool_scale, w_out, seq, tm=256):
    n = x.shape[0]
    tm = min(tm, seq)
    hb = tm // POOL_HALO
    return pl.pallas_call(
        functools.partial(_even_post_kernel, tiles_per_seq=seq // tm),
        grid=(n // tm,),
        in_specs=[pl.BlockSpec((tm, D_MODEL), lambda i: (i, 0)),
                  pl.BlockSpec((tm, FOX_WIDTH), lambda i: (i, 0)),
                  pl.BlockSpec((tm, POOL_WIDTH), lambda i: (i, 0)),
                  pl.BlockSpec((POOL_HALO, POOL_WIDTH), lambda i: (jnp.maximum(i * hb - 1, 0), 0)),
                  _const_spec(pool_w.shape), _const_spec(pool_scale.shape), _const_spec(w_out.shape)],
        out_specs=pl.BlockSpec((tm, D_MODEL), lambda i: (i, 0)),
        out_shape=jax.ShapeDtypeStruct((n, D_MODEL), F32),
        compiler_params=_cparams("parallel"),
    )(x, attn, u, u, pool_w, pool_scale, w_out)


def _ffn_kernel(x_ref, gain_ref, wg_ref, wu_ref, wd_ref, o_ref, hb_ref):
    j = pl.program_id(1)

    @pl.when(j == 0)
    def _():
        x = x_ref[...]
        hb_ref[...] = _rms(x, gain_ref[...]).astype(BF16)
        o_ref[...] = x

    hb = hb_ref[...]
    gate = _dot(hb, wg_ref[...])
    up = _dot(hb, wu_ref[...])
    act = (gate * jax.nn.sigmoid(gate) * up).astype(BF16)
    o_ref[...] += _dot(act, wd_ref[...])


def _ffn(x, gain, w_gate, w_up, w_down, tm=512, tf=1408):
    n = x.shape[0]
    d_ff = w_gate.shape[1]
    tm = min(tm, n)
    return pl.pallas_call(
        _ffn_kernel,
        grid=(n // tm, d_ff // tf),
        in_specs=[pl.BlockSpec((tm, D_MODEL), lambda i, j: (i, 0)),
                  _const_spec(gain.shape),
                  pl.BlockSpec((D_MODEL, tf), lambda i, j: (0, j)),
                  pl.BlockSpec((D_MODEL, tf), lambda i, j: (0, j)),
                  pl.BlockSpec((tf, D_MODEL), lambda i, j: (j, 0))],
        out_specs=pl.BlockSpec((tm, D_MODEL), lambda i, j: (i, 0)),
        out_shape=jax.ShapeDtypeStruct((n, D_MODEL), F32),
        scratch_shapes=[pltpu.VMEM((tm, D_MODEL), BF16)],
        compiler_params=_cparams("parallel", "arbitrary"),
    )(x, gain, w_gate, w_up, w_down)


def _rwkv_proj_kernel(*refs, tiles_per_seq, has_vmix):
    if has_vmix:
        (x_ref, halo_ref, gain_ref, mu_ref, wr_ref, wk_ref, wv_ref, w0_ref, w1_ref, w2_ref,
         a0_ref, a1_ref, a2_ref, g1_ref, g2_ref, vf_ref, v0_ref, v1_ref, v2_ref,
         r_ref, lw_ref, k_ref, v_ref, a_ref, g_ref) = refs
    else:
        (x_ref, halo_ref, gain_ref, mu_ref, wr_ref, wk_ref, wv_ref, w0_ref, w1_ref, w2_ref,
         a0_ref, a1_ref, a2_ref, g1_ref, g2_ref,
         r_ref, lw_ref, k_ref, v_ref, a_ref, g_ref) = refs
    i = pl.program_id(0)
    tm = x_ref.shape[0]
    gain = gain_ref[...]
    h = _rms(x_ref[...], gain)
    h_last = _rms(halo_ref[...], gain)[halo_ref.shape[0] - 1:, :]
    h_last = h_last * (i % tiles_per_seq != 0).astype(F32)
    rows = lax.broadcasted_iota(jnp.int32, (tm, 1), 0)
    h_prev = jnp.where(rows == 0, h_last, pltpu.roll(h, 1, axis=0))
    xx = h_prev - h

    def mix(idx):
        return (h + xx * mu_ref[idx:idx + 1, :]).astype(BF16)

    xr, xw, xk, xv, xa, xg = (mix(idx) for idx in range(6))
    r_ref[...] = _dot(xr, wr_ref[...])
    k_ref[...] = _dot(xk, wk_ref[...])
    v = _dot(xv, wv_ref[...])
    wl = w0_ref[...] + _dot(jnp.tanh(_dot(xw, w1_ref[...])).astype(BF16), w2_ref[...])
    lw_ref[...] = -DECAY_SCALE * jax.nn.sigmoid(wl)
    a_ref[...] = jax.nn.sigmoid(a0_ref[...] + _dot(_dot(xa, a1_ref[...]).astype(BF16), a2_ref[...]))
    g_ref[...] = _dot(jax.nn.sigmoid(_dot(xg, g1_ref[...])).astype(BF16), g2_ref[...])
    if has_vmix:
        gate = jax.nn.sigmoid(v0_ref[...] + _dot(_dot(xv, v1_ref[...]).astype(BF16), v2_ref[...]))
        v = v + (vf_ref[...] - v) * gate
    v_ref[...] = v


def _rwkv_proj(x, gain, p, v_first, seq, tm=256):
    n = x.shape[0]
    tm = min(tm, seq)
    has_vmix = v_first is not None
    tile = pl.BlockSpec((tm, D_MODEL), lambda i: (i, 0))
    halo = pl.BlockSpec((8, D_MODEL), lambda i: (jnp.maximum(i * (tm // 8) - 1, 0), 0))
    consts = [gain, p['mu'], p['w_r'], p['w_k'], p['w_v'], p['w0'], p['w1'], p['w2'],
              p['a0'], p['a1'], p['a2'], p['g1'], p['g2']]
    args = [x, x] + consts
    specs = [tile, halo] + [_const_spec(c.shape) for c in consts]
    if has_vmix:
        extra = [p['v0'], p['v1'], p['v2']]
        args += [v_first] + extra
        specs += [tile] + [_const_spec(c.shape) for c in extra]
    out = jax.ShapeDtypeStruct((n, D_MODEL), F32)
    return pl.pallas_call(
        functools.partial(_rwkv_proj_kernel, tiles_per_seq=seq // tm, has_vmix=has_vmix),
        grid=(n // tm,),
        in_specs=specs,
        out_specs=[tile] * 6,
        out_shape=[out] * 6,
        compiler_params=_cparams("parallel"),
    )(*args)


def _rwkv_scan_kernel(r_ref, lw_ref, k_ref, v_ref, a_ref, g_ref, kk_ref, ka_ref, rk_ref, lnw_ref, lnb_ref,
                      o_ref, state_ref):
    t = pl.program_id(2)

    @pl.when(t == 0)
    def _():
        state_ref[...] = jnp.zeros_like(state_ref)

    L = CHUNK
    chunks = range(r_ref.shape[0] // L)
    first = lax.broadcasted_iota(jnp.int32, (1, LANES), 1) < HEAD_DIM
    trow = lax.broadcasted_iota(jnp.int32, (L, LANES), 0)
    tcol = lax.broadcasted_iota(jnp.int32, (L, LANES), 1) & (HEAD_DIM - 1)
    strict = tcol < trow
    incl = tcol <= trow
    same16 = (trow >> 4) == (tcol >> 4)
    same32 = (trow >> 5) == (tcol >> 5)
    eye = jnp.where(trow == tcol, 1.0, 0.0)
    same_head = ((lax.broadcasted_iota(jnp.int32, (LANES, LANES), 0) >> 6)
                 == (lax.broadcasted_iota(jnp.int32, (LANES, LANES), 1) >> 6))
    tri = jnp.where(lax.broadcasted_iota(jnp.int32, (L, L), 1) <= lax.broadcasted_iota(jnp.int32, (L, L), 0),
                    1.0, 0.0).astype(BF16)

    def bf(z):
        return z.astype(BF16)

    def stack(zb):
        return jnp.concatenate([jnp.where(first, zb, 0), jnp.where(first, 0, zb)], axis=0)

    def wmm(x, y):
        return _dot(bf(x), stack(bf(y)))

    def head_sums(z):
        s0 = jnp.sum(jnp.where(first, z, 0.0), axis=-1, keepdims=True)
        s1 = jnp.sum(jnp.where(first, 0.0, z), axis=-1, keepdims=True)
        return jnp.where(first, s0, s1)

    def load(ref):
        return [ref[c * L:(c + 1) * L, :] for c in chunks]

    r, lw, kraw, v, asig = load(r_ref), load(lw_ref), load(k_ref), load(v_ref), load(a_ref)
    kk = [x * kk_ref[...] for x in kraw]
    kk = [x / jnp.maximum(jnp.sqrt(head_sums(x * x)), 1e-12) for x in kk]
    k = [x * (1.0 + (s - 1.0) * ka_ref[...]) for x, s in zip(kraw, asig)]
    b = [x * s for x, s in zip(kk, asig)]
    lw_hi = [bf(x) for x in lw]
    lw_lo = [bf(x - h.astype(F32)) for x, h in zip(lw, lw_hi)]
    cum = [_dot(tri, jnp.concatenate([h, l], axis=1)) for h, l in zip(lw_hi, lw_lo)]
    cum = [x[:, :LANES] + x[:, LANES:] for x in cum]
    cum_end = [x[L - 1:L, :] for x in cum]
    e_neg = [jnp.exp(-x) for x in cum]
    e_end = [jnp.exp(ce - x) for x, ce in zip(cum, cum_end)]
    rt = [x * jnp.exp(c) for x, c in zip(r, cum)]
    at_b = [bf(-x * jnp.exp(c - w)) for x, c, w in zip(kk, cum, lw)]
    bt = [bf(x * e) for x, e in zip(b, e_neg)]
    kt = [bf(x * e) for x, e in zip(k, e_neg)]
    bh = [bf(x * e) for x, e in zip(b, e_end)]
    kh = [bf(x * e) for x, e in zip(k, e_end)]
    v_b = [bf(x) for x in v]
    aa = [_dot_nt(jnp.concatenate([xa, bf(xr)], axis=0), jnp.concatenate([stack(xb), stack(xk)], axis=0))
          for xa, xr, xb, xk in zip(at_b, rt, bt, kt)]
    a_ab = [jnp.where(strict, x[:L, :LANES], 0.0) for x in aa]
    a_ak = [jnp.where(strict, x[:L, LANES:], 0.0) for x in aa]
    a_rb = [jnp.where(incl, x[L:, :LANES], 0.0) for x in aa]
    a_rk = [jnp.where(incl, x[L:, LANES:], 0.0) for x in aa]
    pw = [jnp.where(same16, x, 0.0) for x in a_ab]
    tinv = [eye + x for x in pw]
    for _ in range(3):
        pw = [wmm(x, x) for x in pw]
        tinv = [x + wmm(x, p) for x, p in zip(tinv, pw)]
    off = [jnp.where(same32 & ~same16, x, 0.0) for x in a_ab]
    tinv = [x + wmm(wmm(x, o), x) for x, o in zip(tinv, off)]
    off = [jnp.where(same32, 0.0, x) for x in a_ab]
    tinv = [x + wmm(wmm(x, o), x) for x, o in zip(tinv, off)]
    akv = [wmm(x, y) for x, y in zip(a_ak, v)]
    x12 = [_dot(bf(x), jnp.concatenate([stack(y), stack(bf(z))], axis=1))
           for x, y, z in zip(tinv, at_b, akv)]
    x1 = [bf(x[:, :LANES]) for x in x12]
    x2 = [bf(x[:, LANES:]) for x in x12]
    ab12 = [_dot(bf(x), jnp.concatenate([stack(y), stack(z)], axis=1)) for x, y, z in zip(a_rb, x1, x2)]
    y1 = [bf(x + y[:, :LANES]) for x, y in zip(rt, ab12)]
    y2 = [y[:, LANES:] + wmm(x, z) for y, x, z in zip(ab12, a_rk, v)]
    m_low = [bf(jnp.where(same_head, _dot_tn(x, y), 0.0)) for x, y in zip(x1, bh)]
    c_full = [_dot_tn(jnp.concatenate([x, y], axis=0), jnp.concatenate([z, w], axis=0))
              for x, y, z, w in zip(x2, v_b, bh, kh)]
    c_wide = [jnp.where(first, x[:L], x[L:]) for x in c_full]
    p_end = [jnp.exp(x) for x in cum_end]

    state = state_ref[...]
    y = []
    for c in chunks:
        state_b = bf(state)
        y.append(_dot_nt(y1[c], stack(state_b)) + y2[c])
        state = state * p_end[c] + _dot(state_b, m_low[c]) + c_wide[c]
    state_ref[...] = state

    for c in chunks:
        mean = head_sums(y[c]) / HEAD_DIM
        dev = y[c] - mean
        var = head_sums(dev * dev) / HEAD_DIM
        z = dev * lax.rsqrt(var + GN_EPS) * lnw_ref[...] + lnb_ref[...]
        bonus = head_sums(r[c] * k[c] * rk_ref[...]) * v[c]
        o_ref[c * L:(c + 1) * L, :] = ((z + bonus) * g_ref[c * L:(c + 1) * L, :]).astype(BF16)


def _rwkv_scan(r, lw, k, v, a, g, p, batch, seq, tb=512):
    n = r.shape[0]
    tb = min(tb, seq)
    nt = seq // tb
    tile = pl.BlockSpec((tb, LANES), lambda b, h, t: (b * nt + t, h))
    lane_const = pl.BlockSpec((1, LANES), lambda b, h, t: (0, h))
    return pl.pallas_call(
        _rwkv_scan_kernel,
        grid=(batch, D_MODEL // LANES, nt),
        in_specs=[tile] * 6 + [lane_const] * 5,
        out_specs=tile,
        out_shape=jax.ShapeDtypeStruct((n, D_MODEL), BF16),
        scratch_shapes=[pltpu.VMEM((HEAD_DIM, LANES), F32)],
        compiler_params=_cparams("parallel", "parallel", "arbitrary"),
    )(r, lw, k, v, a, g, p['k_k'], p['k_a'], p['r_k'], p['ln_w'], p['ln_b'])


def _proj_residual_kernel(x_ref, y_ref, w_ref, o_ref):
    o_ref[...] = x_ref[...] + _dot(y_ref[...], w_ref[...])


def _proj_residual(x, y, w, tm=512):
    n = x.shape[0]
    tm = min(tm, n)
    tile = pl.BlockSpec((tm, D_MODEL), lambda i: (i, 0))
    return pl.pallas_call(
        _proj_residual_kernel,
        grid=(n // tm,),
        in_specs=[tile, tile, _const_spec(w.shape)],
        out_specs=tile,
        out_shape=jax.ShapeDtypeStruct((n, D_MODEL), F32),
        compiler_params=_cparams("parallel"),
    )(x, y, w)


def _pad_to(t, axis, size):
    pad = [(0, 0)] * t.ndim
    pad[axis] = (0, size - t.shape[axis])
    return jnp.pad(t, pad)


def _row(t):
    return t.reshape(1, -1).astype(F32)


def _pair_tile(gain):
    return jnp.tile(gain.astype(F32), 2).reshape(1, LANES)


def kernel(x, mix_norm, ffn_norm, ffn_w_gate, ffn_w_up, ffn_w_down, hy_w_in, hy_f_bias, hy_q_gain, hy_k_gain, hy_pool_w, hy_pool_scale, hy_w_out, rw_mu, rw_w_r, rw_w_k, rw_w_v, rw_w0, rw_w1, rw_w2, rw_a0, rw_a1, rw_a2, rw_g1, rw_g2, rw_k_k, rw_k_a, rw_r_k, rw_ln_w, rw_ln_b, rw_w_o, rw_v0, rw_v1, rw_v2):
    batch, seq, d_model = x.shape
    assert d_model == D_MODEL and seq % 256 == 0
    depth = mix_norm.shape[0]
    n = batch * seq
    xs = x.reshape(n, d_model)
    v_first = None
    for layer in range(depth):
        gain = _row(mix_norm[layer])
        if layer % 2 == 0:
            e = layer // 2
            w_in = hy_w_in[e]
            w_main = jnp.concatenate([w_in[:, :4 * FOX_WIDTH], w_in[:, 4 * FOX_WIDTH + FOX_HEADS:]], axis=1).astype(BF16)
            wf_t = _pad_to(w_in[:, 4 * FOX_WIDTH:4 * FOX_WIDTH + FOX_HEADS].T, 0, 16).astype(BF16)
            f_bias = hy_f_bias[e].reshape(FOX_HEADS, 1).astype(F32)
            qkvg, u, cum = _inproj_even(xs, gain, w_main, wf_t, f_bias, seq)
            attn = _fox_attn(qkvg, cum, _pair_tile(hy_q_gain[e]), _pair_tile(hy_k_gain[e]), batch, seq)
            xs = _even_post(xs, attn, u, hy_pool_w[e].astype(BF16), _row(hy_pool_scale[e]),
                            hy_w_out[e].astype(BF16), seq)
        else:
            o = layer // 2
            p = dict(mu=rw_mu[o].astype(F32),
                     w_r=rw_w_r[o].astype(BF16), w_k=rw_w_k[o].astype(BF16), w_v=rw_w_v[o].astype(BF16),
                     w0=_row(rw_w0[o]), w1=_pad_to(rw_w1[o], 1, LANES).astype(BF16),
                     w2=_pad_to(rw_w2[o], 0, LANES).astype(BF16),
                     a0=_row(rw_a0[o]), a1=_pad_to(rw_a1[o], 1, LANES).astype(BF16),
                     a2=_pad_to(rw_a2[o], 0, LANES).astype(BF16),
                     g1=_pad_to(rw_g1[o], 1, 2 * LANES).astype(BF16),
                     g2=_pad_to(rw_g2[o], 0, 2 * LANES).astype(BF16),
                     k_k=_row(rw_k_k[o]), k_a=_row(rw_k_a[o]), r_k=_row(rw_r_k[o]),
                     ln_w=_row(rw_ln_w[o]), ln_b=_row(rw_ln_b[o]))
            if o > 0:
                p.update(v0=_row(rw_v0[o - 1]), v1=_pad_to(rw_v1[o - 1], 1, LANES).astype(BF16),
                         v2=_pad_to(rw_v2[o - 1], 0, LANES).astype(BF16))
            r, lw, k, v, a, g = _rwkv_proj(xs, gain, p, v_first if o > 0 else None, seq)
            if o == 0:
                v_first = v
            y = _rwkv_scan(r, lw, k, v, a, g, p, batch, seq)
            xs = _proj_residual(xs, y, rw_w_o[o].astype(BF16))
        xs = _ffn(xs, _row(ffn_norm[layer]), ffn_w_gate[layer].astype(BF16), ffn_w_up[layer].astype(BF16),
                  ffn_w_down[layer].astype(BF16))
    return xs.reshape(batch, seq, d_model)
```

```python
import functools

import jax
import jax.numpy as jnp
from jax import lax
from jax.experimental import pallas as pl
from jax.experimental.pallas import tpu as pltpu

F32 = jnp.float32
BF16 = jnp.bfloat16

D_MODEL = 1024
HEAD_DIM = 64
LANES = 128
FOX_HEADS = 8
FOX_WIDTH = FOX_HEADS * HEAD_DIM
FOX_PAIRS = FOX_HEADS // 2
POOL_WINDOWS = (2, 4, 8, 16)
POOL_WIDTH = D_MODEL - FOX_WIDTH
POOL_HALO = 16
RMS_EPS = 1e-6
GN_EPS = 64e-5
CHUNK = 64
DECAY_SCALE = 0.6065306597126334
NEG_BIG = -1e30
LOG2E = 1.4426950408889634
VMEM_LIMIT = 56 * 1024 * 1024


def _cparams(*sem):
    return pltpu.CompilerParams(dimension_semantics=sem, vmem_limit_bytes=VMEM_LIMIT)


def _rms(x, gain):
    ms = jnp.mean(x * x, axis=-1, keepdims=True)
    return x * lax.rsqrt(ms + RMS_EPS) * gain


def _dot(a, b):
    return jnp.dot(a, b, preferred_element_type=F32)


def _dot_nt(a, b):
    return lax.dot_general(a, b, (((1,), (1,)), ((), ())), preferred_element_type=F32)


def _dot_tn(a, b):
    return lax.dot_general(a, b, (((0,), (0,)), ((), ())), preferred_element_type=F32)


def _const_spec(shape):
    nd = len(shape)
    return pl.BlockSpec(shape, lambda *_: (0,) * nd, pipeline_mode=pl.Buffered(1))


def _inproj_even_kernel(x_ref, gain_ref, w_ref, wf_ref, fb_ref,
                        qkvg_ref, u_ref, cum_ref, carry_ref, *, tiles_per_seq):
    i = pl.program_id(0)
    hb = _rms(x_ref[...], gain_ref[...]).astype(BF16)
    main = _dot(hb, w_ref[...])
    qkvg_ref[...] = main[:, :4 * FOX_WIDTH].astype(BF16)
    u_ref[...] = main[:, 4 * FOX_WIDTH:]
    ft = _dot_nt(wf_ref[...], hb)[:FOX_HEADS]
    z = ft + fb_ref[...]
    c = jnp.minimum(z, 0.0) - jnp.log1p(jnp.exp(-jnp.abs(z)))
    tm = c.shape[1]
    lane = lax.broadcasted_iota(jnp.int32, c.shape, 1)
    sh = 1
    while sh < tm:
        c = c + jnp.where(lane >= sh, pltpu.roll(c, sh, axis=1), 0.0)
        sh *= 2

    @pl.when(i % tiles_per_seq == 0)
    def _():
        carry_ref[...] = jnp.zeros_like(carry_ref)

    c = c + carry_ref[...]
    cum_ref[...] = c
    carry_ref[...] = c[:, tm - 1:tm]


def _inproj_even(x, gain, w_main, wf_t, f_bias, seq, tm=512):
    n = x.shape[0]
    tm = min(tm, seq)
    return pl.pallas_call(
        functools.partial(_inproj_even_kernel, tiles_per_seq=seq // tm),
        grid=(n // tm,),
        in_specs=[pl.BlockSpec((tm, D_MODEL), lambda i: (i, 0)),
                  _const_spec(gain.shape), _const_spec(w_main.shape),
                  _const_spec(wf_t.shape), _const_spec(f_bias.shape)],
        out_specs=[pl.BlockSpec((tm, 4 * FOX_WIDTH), lambda i: (i, 0)),
                   pl.BlockSpec((tm, POOL_WIDTH), lambda i: (i, 0)),
                   pl.BlockSpec((FOX_HEADS, tm), lambda i: (0, i))],
        out_shape=[jax.ShapeDtypeStruct((n, 4 * FOX_WIDTH), BF16),
                   jax.ShapeDtypeStruct((n, POOL_WIDTH), F32),
                   jax.ShapeDtypeStruct((FOX_HEADS, n), F32)],
        scratch_shapes=[pltpu.VMEM((FOX_HEADS, 1), F32)],
        compiler_params=_cparams("arbitrary"),
    )(x, gain, w_main, wf_t, f_bias)


def _pair_rms_scale(t, first):
    t2 = t * t
    s0 = jnp.sum(jnp.where(first, t2, 0.0), axis=-1, keepdims=True)
    s1 = jnp.sum(jnp.where(first, 0.0, t2), axis=-1, keepdims=True)
    return jnp.where(first, lax.rsqrt(s0 / HEAD_DIM + RMS_EPS), lax.rsqrt(s1 / HEAD_DIM + RMS_EPS))


def _fox_attn_kernel(q_ref, k_ref, v_ref, og_ref, cum_ref, qg_ref, kg_ref, o_ref, kn_ref, *, tq, seq):
    qi = pl.program_id(2)
    first = lax.broadcasted_iota(jnp.int32, (1, LANES), 1) < HEAD_DIM

    @pl.when(qi == 0)
    def _():
        def norm_rows(t, carry):
            off = pl.multiple_of(t * tq, tq)
            k = k_ref[pl.ds(off, tq), :].astype(F32)
            kn_ref[pl.ds(off, tq), :] = (k * _pair_rms_scale(k, first) * kg_ref[...]).astype(BF16)
            return carry
        lax.fori_loop(0, seq // tq, norm_rows, 0)

    q = q_ref[...].astype(F32)
    qn = q * _pair_rms_scale(q, first) * (qg_ref[...] * (HEAD_DIM ** -0.5 * LOG2E))
    q_heads = (jnp.where(first, qn, 0.0).astype(BF16), jnp.where(first, 0.0, qn).astype(BF16))
    c_q = cum_ref[:, pl.ds(pl.multiple_of(qi * tq, tq), tq)]
    c_ref = (c_q[0:1, 0:1] * LOG2E, c_q[1:2, 0:1] * LOG2E)
    row = lax.broadcasted_iota(jnp.int32, (tq, tq), 0)
    col = lax.broadcasted_iota(jnp.int32, (tq, tq), 1)
    causal = col <= row

    def step(j, stats, diagonal):
        off = pl.multiple_of(j * tq, tq)
        kb = kn_ref[pl.ds(off, tq), :]
        vb = v_ref[pl.ds(off, tq), :]
        cb = cum_ref[:, pl.ds(off, tq)] * LOG2E
        new = []
        for h in range(2):
            m_old, l_old, acc = stats[h]
            s = _dot_nt(q_heads[h], kb) + (c_ref[h] - cb[h:h + 1, :])
            if diagonal:
                s = jnp.where(causal, s, NEG_BIG)
            m_new = jnp.maximum(m_old, jnp.max(s, axis=-1, keepdims=True))
            alpha = jnp.exp2(m_old - m_new)
            p = jnp.exp2(s - m_new)
            l_new = alpha * l_old + jnp.sum(p, axis=-1, keepdims=True)
            acc = alpha * acc + _dot(p.astype(BF16), vb)
            new.append((m_new, l_new, acc))
        return tuple(new)

    init = tuple((jnp.full((tq, 1), NEG_BIG, F32), jnp.zeros((tq, 1), F32), jnp.zeros((tq, LANES), F32))
                 for _ in range(2))
    stats = lax.fori_loop(0, qi, lambda j, c: step(j, c, False), init)
    (_, l0, a0), (_, l1, a1) = step(qi, stats, True)
    attn = jnp.where(first, a0 / l0, a1 / l1)
    o_ref[...] = (attn * jax.nn.sigmoid(og_ref[...].astype(F32))).astype(BF16)


def _fox_attn(qkvg, cum, q_gain2, k_gain2, batch, seq, tq=512):
    n = qkvg.shape[0]
    tq = min(tq, seq)
    nq = seq // tq
    cum_pairs = cum.reshape(FOX_PAIRS, 2, n)
    return pl.pallas_call(
        functools.partial(_fox_attn_kernel, tq=tq, seq=seq),
        grid=(batch, FOX_PAIRS, nq),
        in_specs=[pl.BlockSpec((tq, LANES), lambda b, p, i: (b * nq + i, p)),
                  pl.BlockSpec((seq, LANES), lambda b, p, i: (b, FOX_PAIRS + p)),
                  pl.BlockSpec((seq, LANES), lambda b, p, i: (b, 2 * FOX_PAIRS + p)),
                  pl.BlockSpec((tq, LANES), lambda b, p, i: (b * nq + i, 3 * FOX_PAIRS + p)),
                  pl.BlockSpec((None, 2, seq), lambda b, p, i: (p, 0, b)),
                  _const_spec(q_gain2.shape), _const_spec(k_gain2.shape)],
        out_specs=pl.BlockSpec((tq, LANES), lambda b, p, i: (b * nq + i, p)),
        out_shape=jax.ShapeDtypeStruct((n, FOX_WIDTH), BF16),
        scratch_shapes=[pltpu.VMEM((seq, LANES), BF16)],
        compiler_params=_cparams("parallel", "parallel", "arbitrary"),
    )(qkvg, qkvg, qkvg, qkvg, cum_pairs, q_gain2, k_gain2)


def _split_bf16(t):
    hi = t.astype(BF16)
    lo = (t - hi.astype(F32)).astype(BF16)
    return hi, lo


def _pool_kernel(u_ref, halo_ref, pw_ref, ps_ref, o_ref, *, tiles_per_seq):
    i = pl.program_id(0)
    tm = u_ref.shape[0]
    tile_in_seq = i % tiles_per_seq
    u = u_ref[...]
    halo = halo_ref[...] * (tile_in_seq != 0).astype(F32)
    row = lax.broadcasted_iota(jnp.int32, (tm, tm), 0)
    col = lax.broadcasted_iota(jnp.int32, (tm, tm), 1)
    hrow = lax.broadcasted_iota(jnp.int32, (tm, POOL_HALO), 0)
    hcol = lax.broadcasted_iota(jnp.int32, (tm, POOL_HALO), 1)
    pos = tile_in_seq * tm + lax.broadcasted_iota(jnp.int32, (tm, 1), 0)
    for g, w in enumerate(POOL_WINDOWS):
        sl = slice(g * LANES, (g + 1) * LANES)
        band = jnp.where((col <= row) & (row - col < w), 1.0, 0.0).astype(BF16)
        hband = jnp.where(hrow + POOL_HALO - hcol < w, 1.0, 0.0).astype(BF16)
        u_hi, u_lo = _split_bf16(u[:, sl])
        h_hi, h_lo = _split_bf16(halo[:, sl])
        wsum = _dot(band, u_hi) + _dot(band, u_lo) + _dot(hband, h_hi) + _dot(hband, h_lo)
        count = jnp.minimum(pos + 1, w).astype(F32)
        pooled = wsum / count - u[:, sl]
        o_ref[:, sl] = (_dot(pooled.astype(BF16), pw_ref[g]) * ps_ref[:, sl]).astype(BF16)


def _pool(u, pool_w, pool_scale, seq, tm=256):
    n = u.shape[0]
    tm = min(tm, seq)
    hb = tm // POOL_HALO
    return pl.pallas_call(
        functools.partial(_pool_kernel, tiles_per_seq=seq // tm),
        grid=(n // tm,),
        in_specs=[pl.BlockSpec((tm, POOL_WIDTH), lambda i: (i, 0)),
                  pl.BlockSpec((POOL_HALO, POOL_WIDTH), lambda i: (jnp.maximum(i * hb - 1, 0), 0)),
                  _const_spec(pool_w.shape), _const_spec(pool_scale.shape)],
        out_specs=pl.BlockSpec((tm, POOL_WIDTH), lambda i: (i, 0)),
        out_shape=jax.ShapeDtypeStruct((n, POOL_WIDTH), BF16),
        compiler_params=_cparams("parallel"),
    )(u, u, pool_w, pool_scale)


def _proj_ffn_kernel(*refs, n_mix):
    x_ref = refs[0]
    mix_refs = refs[1:1 + 2 * n_mix]
    gain_ref, wg_ref, wu_ref, wd_ref, o_ref = refs[1 + 2 * n_mix:]
    x = x_ref[...]
    for m in range(n_mix):
        x = x + _dot(mix_refs[2 * m][...], mix_refs[2 * m + 1][...])
    hb = _rms(x, gain_ref[...]).astype(BF16)
    gate = _dot(hb, wg_ref[...])
    up = _dot(hb, wu_ref[...])
    act = (gate * jax.nn.sigmoid(gate) * up).astype(BF16)
    o_ref[...] = x + _dot(act, wd_ref[...])


def _proj_ffn(x, mixes, gain, w_gate, w_up, w_down, tm=512):
    n = x.shape[0]
    tm = min(tm, n)
    args, specs = [x], [pl.BlockSpec((tm, D_MODEL), lambda i: (i, 0))]
    for y, w in mixes:
        args += [y, w]
        specs += [pl.BlockSpec((tm, y.shape[1]), lambda i: (i, 0)), _const_spec(w.shape)]
    consts = [gain, w_gate, w_up, w_down]
    return pl.pallas_call(
        functools.partial(_proj_ffn_kernel, n_mix=len(mixes)),
        grid=(n // tm,),
        in_specs=specs + [_const_spec(c.shape) for c in consts],
        out_specs=pl.BlockSpec((tm, D_MODEL), lambda i: (i, 0)),
        out_shape=jax.ShapeDtypeStruct((n, D_MODEL), F32),
        compiler_params=_cparams("parallel"),
    )(*args, *consts)


def _rwkv_proj_kernel(*refs, tiles_per_seq, has_vmix):
    if has_vmix:
        (x_ref, halo_ref, gain_ref, mu_ref, wr_ref, wk_ref, wv_ref, w0_ref, w1_ref, w2_ref,
         a0_ref, a1_ref, a2_ref, g1_ref, g2_ref, vf_ref, v0_ref, v1_ref, v2_ref,
         r_ref, lw_ref, k_ref, v_ref, a_ref, g_ref) = refs
    else:
        (x_ref, halo_ref, gain_ref, mu_ref, wr_ref, wk_ref, wv_ref, w0_ref, w1_ref, w2_ref,
         a0_ref, a1_ref, a2_ref, g1_ref, g2_ref,
         r_ref, lw_ref, k_ref, v_ref, a_ref, g_ref) = refs
    i = pl.program_id(0)
    tm = x_ref.shape[0]
    gain = gain_ref[...]
    h = _rms(x_ref[...], gain)
    h_last = _rms(halo_ref[...], gain)[halo_ref.shape[0] - 1:, :]
    h_last = h_last * (i % tiles_per_seq != 0).astype(F32)
    rows = lax.broadcasted_iota(jnp.int32, (tm, 1), 0)
    h_prev = jnp.where(rows == 0, h_last, pltpu.roll(h, 1, axis=0))
    xx = h_prev - h

    def mix(idx):
        return (h + xx * mu_ref[idx:idx + 1, :]).astype(BF16)

    xr, xw, xk, xv, xa, xg = (mix(idx) for idx in range(6))
    r_ref[...] = _dot(xr, wr_ref[...]).astype(BF16)
    k_ref[...] = _dot(xk, wk_ref[...]).astype(BF16)
    v = _dot(xv, wv_ref[...])
    wl = w0_ref[...] + _dot(jnp.tanh(_dot(xw, w1_ref[...])).astype(BF16), w2_ref[...])
    lw_ref[...] = -DECAY_SCALE * jax.nn.sigmoid(wl)
    a_ref[...] = jax.nn.sigmoid(
        a0_ref[...] + _dot(_dot(xa, a1_ref[...]).astype(BF16), a2_ref[...])).astype(BF16)
    g_ref[...] = _dot(jax.nn.sigmoid(_dot(xg, g1_ref[...])).astype(BF16), g2_ref[...]).astype(BF16)
    if has_vmix:
        gate = jax.nn.sigmoid(v0_ref[...] + _dot(_dot(xv, v1_ref[...]).astype(BF16), v2_ref[...]))
        v = v + (vf_ref[...].astype(F32) - v) * gate
    v_ref[...] = v.astype(BF16)


def _rwkv_proj(x, gain, p, v_first, seq, tm=256):
    n = x.shape[0]
    tm = min(tm, seq)
    has_vmix = v_first is not None
    tile = pl.BlockSpec((tm, D_MODEL), lambda i: (i, 0))
    halo = pl.BlockSpec((8, D_MODEL), lambda i: (jnp.maximum(i * (tm // 8) - 1, 0), 0))
    consts = [gain, p['mu'], p['w_r'], p['w_k'], p['w_v'], p['w0'], p['w1'], p['w2'],
              p['a0'], p['a1'], p['a2'], p['g1'], p['g2']]
    args = [x, x] + consts
    specs = [tile, halo] + [_const_spec(c.shape) for c in consts]
    if has_vmix:
        extra = [p['v0'], p['v1'], p['v2']]
        args += [v_first] + extra
        specs += [tile] + [_const_spec(c.shape) for c in extra]
    out_dtypes = [BF16, F32, BF16, BF16, BF16, BF16]
    return pl.pallas_call(
        functools.partial(_rwkv_proj_kernel, tiles_per_seq=seq // tm, has_vmix=has_vmix),
        grid=(n // tm,),
        in_specs=specs,
        out_specs=[tile] * 6,
        out_shape=[jax.ShapeDtypeStruct((n, D_MODEL), dt) for dt in out_dtypes],
        compiler_params=_cparams("parallel"),
    )(*args)


def _rwkv_scan_kernel(r_ref, lw_ref, k_ref, v_ref, a_ref, g_ref, kk_ref, ka_ref, rk_ref, lnw_ref, lnb_ref,
                      o_ref, state_ref):
    t = pl.program_id(2)

    @pl.when(t == 0)
    def _():
        state_ref[...] = jnp.zeros_like(state_ref)

    L = CHUNK
    n_chunks = r_ref.shape[0] // L
    n_pairs = r_ref.shape[1] // LANES
    items = [(c, p) for c in range(n_chunks) for p in range(n_pairs)]
    first = lax.broadcasted_iota(jnp.int32, (1, LANES), 1) < HEAD_DIM
    trow = lax.broadcasted_iota(jnp.int32, (L, LANES), 0)
    tcol = lax.broadcasted_iota(jnp.int32, (L, LANES), 1) & (HEAD_DIM - 1)
    strict = tcol < trow
    incl = tcol <= trow
    same16 = (trow >> 4) == (tcol >> 4)
    same32 = (trow >> 5) == (tcol >> 5)
    eye = jnp.where(trow == tcol, 1.0, 0.0)
    same_head = ((lax.broadcasted_iota(jnp.int32, (LANES, LANES), 0) >> 6)
                 == (lax.broadcasted_iota(jnp.int32, (LANES, LANES), 1) >> 6))
    tri = jnp.where(lax.broadcasted_iota(jnp.int32, (L, L), 1) <= lax.broadcasted_iota(jnp.int32, (L, L), 0),
                    1.0, 0.0).astype(BF16)

    def bf(z):
        return z.astype(BF16)

    def stack(zb):
        return jnp.concatenate([jnp.where(first, zb, 0), jnp.where(first, 0, zb)], axis=0)

    def wmm(x, y):
        return _dot(bf(x), stack(bf(y)))

    def head_sums(z):
        s0 = jnp.sum(jnp.where(first, z, 0.0), axis=-1, keepdims=True)
        s1 = jnp.sum(jnp.where(first, 0.0, z), axis=-1, keepdims=True)
        return jnp.where(first, s0, s1)

    def tile(ref, c, p):
        return ref[c * L:(c + 1) * L, p * LANES:(p + 1) * LANES]

    def load(ref):
        return [tile(ref, c, p).astype(F32) for c, p in items]

    def lane_param(ref):
        return [ref[:, p * LANES:(p + 1) * LANES] for _, p in items]

    r, lw, kraw, v, asig = load(r_ref), load(lw_ref), load(k_ref), load(v_ref), load(a_ref)
    kk = [x * w for x, w in zip(kraw, lane_param(kk_ref))]
    kk = [x / jnp.maximum(jnp.sqrt(head_sums(x * x)), 1e-12) for x in kk]
    k = [x * (1.0 + (s - 1.0) * w) for x, s, w in zip(kraw, asig, lane_param(ka_ref))]
    b = [x * s for x, s in zip(kk, asig)]
    lw_hi = [bf(x) for x in lw]
    lw_lo = [bf(x - h.astype(F32)) for x, h in zip(lw, lw_hi)]
    cum = [_dot(tri, jnp.concatenate([h, l], axis=1)) for h, l in zip(lw_hi, lw_lo)]
    cum = [x[:, :LANES] + x[:, LANES:] for x in cum]
    cum_end = [x[L - 1:L, :] for x in cum]
    e_neg = [jnp.exp(-x) for x in cum]
    e_end = [jnp.exp(ce - x) for x, ce in zip(cum, cum_end)]
    rt = [x * jnp.exp(c) for x, c in zip(r, cum)]
    at_b = [bf(-x * jnp.exp(c - w)) for x, c, w in zip(kk, cum, lw)]
    bt = [bf(x * e) for x, e in zip(b, e_neg)]
    kt = [bf(x * e) for x, e in zip(k, e_neg)]
    bh = [bf(x * e) for x, e in zip(b, e_end)]
    kh = [bf(x * e) for x, e in zip(k, e_end)]
    v_b = [bf(x) for x in v]
    aa = [_dot_nt(jnp.concatenate([xa, bf(xr)], axis=0), jnp.concatenate([stack(xb), stack(xk)], axis=0))
          for xa, xr, xb, xk in zip(at_b, rt, bt, kt)]
    a_ab = [jnp.where(strict, x[:L, :LANES], 0.0) for x in aa]
    a_ak = [jnp.where(strict, x[:L, LANES:], 0.0) for x in aa]
    a_rb = [jnp.where(incl, x[L:, :LANES], 0.0) for x in aa]
    a_rk = [jnp.where(incl, x[L:, LANES:], 0.0) for x in aa]
    pw = [jnp.where(same16, x, 0.0) for x in a_ab]
    tinv = [eye + x for x in pw]
    for _ in range(3):
        pw = [wmm(x, x) for x in pw]
        tinv = [x + wmm(x, p) for x, p in zip(tinv, pw)]
    off = [jnp.where(same32 & ~same16, x, 0.0) for x in a_ab]
    tinv = [x + wmm(wmm(x, o), x) for x, o in zip(tinv, off)]
    off = [jnp.where(same32, 0.0, x) for x in a_ab]
    tinv = [x + wmm(wmm(x, o), x) for x, o in zip(tinv, off)]
    akv = [wmm(x, y) for x, y in zip(a_ak, v)]
    x12 = [_dot(bf(x), jnp.concatenate([stack(y), stack(bf(z))], axis=1))
           for x, y, z in zip(tinv, at_b, akv)]
    x1 = [bf(x[:, :LANES]) for x in x12]
    x2 = [bf(x[:, LANES:]) for x in x12]
    ab12 = [_dot(bf(x), jnp.concatenate([stack(y), stack(z)], axis=1)) for x, y, z in zip(a_rb, x1, x2)]
    y1 = [bf(x + y[:, :LANES]) for x, y in zip(rt, ab12)]
    y2 = [y[:, LANES:] + wmm(x, z) for y, x, z in zip(ab12, a_rk, v)]
    m_low = [bf(jnp.where(same_head, _dot_tn(x, y), 0.0)) for x, y in zip(x1, bh)]
    c_full = [_dot_tn(jnp.concatenate([x, y], axis=0), jnp.concatenate([z, w], axis=0))
              for x, y, z, w in zip(x2, v_b, bh, kh)]
    c_wide = [jnp.where(first, x[:L], x[L:]) for x in c_full]
    p_end = [jnp.exp(x) for x in cum_end]

    states = [state_ref[:, p * LANES:(p + 1) * LANES] for p in range(n_pairs)]
    y = []
    for i, (c, p) in enumerate(items):
        state_b = bf(states[p])
        y.append(_dot_nt(y1[i], stack(state_b)) + y2[i])
        states[p] = states[p] * p_end[i] + _dot(state_b, m_low[i]) + c_wide[i]
    for p in range(n_pairs):
        state_ref[:, p * LANES:(p + 1) * LANES] = states[p]

    lnw, lnb, rk = lane_param(lnw_ref), lane_param(lnb_ref), lane_param(rk_ref)
    for i, (c, p) in enumerate(items):
        mean = head_sums(y[i]) / HEAD_DIM
        dev = y[i] - mean
        var = head_sums(dev * dev) / HEAD_DIM
        z = dev * lax.rsqrt(var + GN_EPS) * lnw[i] + lnb[i]
        bonus = head_sums(r[i] * k[i] * rk[i]) * v[i]
        o_ref[c * L:(c + 1) * L, p * LANES:(p + 1) * LANES] = (
            (z + bonus) * tile(g_ref, c, p).astype(F32)).astype(BF16)


def _rwkv_scan(r, lw, k, v, a, g, p, batch, seq, tb=256, lanes=4 * LANES):
    n = r.shape[0]
    tb = min(tb, seq)
    nt = seq // tb
    tile = pl.BlockSpec((tb, lanes), lambda b, h, t: (b * nt + t, h))
    lane_const = pl.BlockSpec((1, lanes), lambda b, h, t: (0, h))
    return pl.pallas_call(
        _rwkv_scan_kernel,
        grid=(batch, D_MODEL // lanes, nt),
        in_specs=[tile] * 6 + [lane_const] * 5,
        out_specs=tile,
        out_shape=jax.ShapeDtypeStruct((n, D_MODEL), BF16),
        scratch_shapes=[pltpu.VMEM((HEAD_DIM, lanes), F32)],
        compiler_params=_cparams("parallel", "parallel", "arbitrary"),
    )(r, lw, k, v, a, g, p['k_k'], p['k_a'], p['r_k'], p['ln_w'], p['ln_b'])


def _pad_to(t, axis, size):
    pad = [(0, 0)] * t.ndim
    pad[axis] = (0, size - t.shape[axis])
    return jnp.pad(t, pad)


def _row(t):
    return t.reshape(1, -1).astype(F32)


def _pair_tile(gain):
    return jnp.tile(gain.astype(F32), 2).reshape(1, LANES)


def kernel(x, mix_norm, ffn_norm, ffn_w_gate, ffn_w_up, ffn_w_down, hy_w_in, hy_f_bias, hy_q_gain, hy_k_gain, hy_pool_w, hy_pool_scale, hy_w_out, rw_mu, rw_w_r, rw_w_k, rw_w_v, rw_w0, rw_w1, rw_w2, rw_a0, rw_a1, rw_a2, rw_g1, rw_g2, rw_k_k, rw_k_a, rw_r_k, rw_ln_w, rw_ln_b, rw_w_o, rw_v0, rw_v1, rw_v2):
    batch, seq, d_model = x.shape
    assert d_model == D_MODEL and seq % 256 == 0
    depth = mix_norm.shape[0]
    n = batch * seq
    xs = x.reshape(n, d_model)
    v_first = None
    for layer in range(depth):
        gain = _row(mix_norm[layer])
        if layer % 2 == 0:
            e = layer // 2
            w_in = hy_w_in[e]
            w_main = jnp.concatenate([w_in[:, :4 * FOX_WIDTH], w_in[:, 4 * FOX_WIDTH + FOX_HEADS:]], axis=1).astype(BF16)
            wf_t = _pad_to(w_in[:, 4 * FOX_WIDTH:4 * FOX_WIDTH + FOX_HEADS].T, 0, 16).astype(BF16)
            f_bias = hy_f_bias[e].reshape(FOX_HEADS, 1).astype(F32)
            qkvg, u, cum = _inproj_even(xs, gain, w_main, wf_t, f_bias, seq)
            attn = _fox_attn(qkvg, cum, _pair_tile(hy_q_gain[e]), _pair_tile(hy_k_gain[e]), batch, seq)
            mixed = _pool(u, hy_pool_w[e].astype(BF16), _row(hy_pool_scale[e]), seq)
            w_out = hy_w_out[e].astype(BF16)
            mixes = [(attn, w_out[:FOX_WIDTH]), (mixed, w_out[FOX_WIDTH:])]
        else:
            o = layer // 2
            p = dict(mu=rw_mu[o].astype(F32),
                     w_r=rw_w_r[o].astype(BF16), w_k=rw_w_k[o].astype(BF16), w_v=rw_w_v[o].astype(BF16),
                     w0=_row(rw_w0[o]), w1=_pad_to(rw_w1[o], 1, LANES).astype(BF16),
                     w2=_pad_to(rw_w2[o], 0, LANES).astype(BF16),
                     a0=_row(rw_a0[o]), a1=_pad_to(rw_a1[o], 1, LANES).astype(BF16),
                     a2=_pad_to(rw_a2[o], 0, LANES).astype(BF16),
                     g1=_pad_to(rw_g1[o], 1, 2 * LANES).astype(BF16),
                     g2=_pad_to(rw_g2[o], 0, 2 * LANES).astype(BF16),
                     k_k=_row(rw_k_k[o]), k_a=_row(rw_k_a[o]), r_k=_row(rw_r_k[o]),
                     ln_w=_row(rw_ln_w[o]), ln_b=_row(rw_ln_b[o]))
            if o > 0:
                p.update(v0=_row(rw_v0[o - 1]), v1=_pad_to(rw_v1[o - 1], 1, LANES).astype(BF16),
                         v2=_pad_to(rw_v2[o - 1], 0, LANES).astype(BF16))
            r, lw, k, v, a, g = _rwkv_proj(xs, gain, p, v_first if o > 0 else None, seq)
            if o == 0:
                v_first = v
            mixes = [(_rwkv_scan(r, lw, k, v, a, g, p, batch, seq), rw_w_o[o].astype(BF16))]
        xs = _proj_ffn(xs, mixes, _row(ffn_norm[layer]), ffn_w_gate[layer].astype(BF16),
                       ffn_w_up[layer].astype(BF16), ffn_w_down[layer].astype(BF16))
    return xs.reshape(batch, seq, d_model)
```

```python
import functools

import jax
import jax.numpy as jnp
from jax import lax
from jax.experimental import pallas as pl
from jax.experimental.pallas import tpu as pltpu

F32 = jnp.float32
BF16 = jnp.bfloat16

D_MODEL = 1024
HEAD_DIM = 64
LANES = 128
FOX_HEADS = 8
FOX_WIDTH = FOX_HEADS * HEAD_DIM
FOX_PAIRS = FOX_HEADS // 2
POOL_WINDOWS = (2, 4, 8, 16)
POOL_WIDTH = D_MODEL - FOX_WIDTH
POOL_HALO = 16
RMS_EPS = 1e-6
GN_EPS = 64e-5
CHUNK = 64
DECAY_SCALE = 0.6065306597126334
NEG_BIG = -1e30
LOG2E = 1.4426950408889634
ATTN_SUB = 512
VMEM_LIMIT = 56 * 1024 * 1024


def _cparams(*sem, flags=None):
    return pltpu.CompilerParams(dimension_semantics=sem, vmem_limit_bytes=VMEM_LIMIT, flags=flags)


def _rms(x, gain):
    ms = jnp.mean(x * x, axis=-1, keepdims=True)
    return x * lax.rsqrt(ms + RMS_EPS) * gain


def _dot(a, b):
    return jnp.dot(a, b, preferred_element_type=F32)


def _dot_nt(a, b):
    return lax.dot_general(a, b, (((1,), (1,)), ((), ())), preferred_element_type=F32)


def _dot_tn(a, b):
    return lax.dot_general(a, b, (((0,), (0,)), ((), ())), preferred_element_type=F32)


def _const_spec(shape):
    nd = len(shape)
    return pl.BlockSpec(shape, lambda *_: (0,) * nd, pipeline_mode=pl.Buffered(1))


def _inproj_even_kernel(x_ref, gain_ref, w_ref, wf_ref, fb_ref,
                        qkvg_ref, u_ref, cum_ref, carry_ref, *, tiles_per_seq):
    i = pl.program_id(0)
    hb = _rms(x_ref[...], gain_ref[...]).astype(BF16)
    main = _dot(hb, w_ref[...])
    qkvg_ref[...] = main[:, :4 * FOX_WIDTH].astype(BF16)
    u_ref[...] = main[:, 4 * FOX_WIDTH:]
    z = _dot(hb, wf_ref[...]) + fb_ref[...]
    c = jnp.minimum(z, 0.0) - jnp.log1p(jnp.exp(-jnp.abs(z)))
    tm = c.shape[0]
    row = lax.broadcasted_iota(jnp.int32, c.shape, 0)
    sh = 1
    while sh < tm:
        c = c + jnp.where(row >= sh, pltpu.roll(c, sh, axis=0), 0.0)
        sh *= 2

    @pl.when(i % tiles_per_seq == 0)
    def _():
        carry_ref[...] = jnp.zeros_like(carry_ref)

    c = c + carry_ref[...]
    cum_ref[...] = c
    carry_ref[...] = c[tm - 1:tm, :]


def _inproj_even(x, gain, w_main, wf, f_bias, seq, tm=512):
    n = x.shape[0]
    tm = min(tm, seq)
    return pl.pallas_call(
        functools.partial(_inproj_even_kernel, tiles_per_seq=seq // tm),
        grid=(n // tm,),
        in_specs=[pl.BlockSpec((tm, D_MODEL), lambda i: (i, 0)),
                  _const_spec(gain.shape), _const_spec(w_main.shape),
                  _const_spec(wf.shape), _const_spec(f_bias.shape)],
        out_specs=[pl.BlockSpec((tm, 4 * FOX_WIDTH), lambda i: (i, 0)),
                   pl.BlockSpec((tm, POOL_WIDTH), lambda i: (i, 0)),
                   pl.BlockSpec((tm, LANES), lambda i: (i, 0))],
        out_shape=[jax.ShapeDtypeStruct((n, 4 * FOX_WIDTH), BF16),
                   jax.ShapeDtypeStruct((n, POOL_WIDTH), F32),
                   jax.ShapeDtypeStruct((n, LANES), F32)],
        scratch_shapes=[pltpu.VMEM((1, LANES), F32)],
        compiler_params=_cparams("arbitrary"),
    )(x, gain, w_main, wf, f_bias)


def _pair_rms_scale(t, first):
    t2 = t * t
    s0 = jnp.sum(jnp.where(first, t2, 0.0), axis=-1, keepdims=True)
    s1 = jnp.sum(jnp.where(first, 0.0, t2), axis=-1, keepdims=True)
    return jnp.where(first, lax.rsqrt(s0 / HEAD_DIM + RMS_EPS), lax.rsqrt(s1 / HEAD_DIM + RMS_EPS))


def _fox_attn_kernel(q_ref, k_ref, v_ref, og_ref, cum_ref, qg_ref, kg_ref, o_ref,
                     kn_ref, vt_ref, bias_ref, *, tq, seq):
    pair = pl.program_id(1)
    qi = pl.program_id(2)
    lane = lax.broadcasted_iota(jnp.int32, (1, LANES), 1)
    first = lane < HEAD_DIM

    @pl.when(qi == 0)
    def _():
        def prepare(t, carry):
            rows = pl.ds(pl.multiple_of(t * tq, tq), tq)
            k = k_ref[rows, :].astype(F32)
            kn_ref[rows, :] = (k * _pair_rms_scale(k, first) * kg_ref[...]).astype(BF16)
            vt_ref[:, rows] = v_ref[rows, :].astype(F32).T.astype(BF16)
            c = cum_ref[rows, :]
            for h in range(2):
                col = jnp.sum(jnp.where(lane == 2 * pair + h, c, 0.0), axis=-1, keepdims=True)
                bias_ref[h, rows, :] = jnp.broadcast_to(col * (-LOG2E), (tq, LANES))
            return carry
        lax.fori_loop(0, seq // tq, prepare, 0)

    q = q_ref[...].astype(F32)
    qn = q * _pair_rms_scale(q, first) * (qg_ref[...] * (HEAD_DIM ** -0.5 * LOG2E))
    q_heads = (jnp.where(first, qn, 0.0).astype(BF16), jnp.where(first, 0.0, qn).astype(BF16))
    n_sub = tq // ATTN_SUB
    chains = [(u, h) for u in range(n_sub) for h in range(2)]
    q_sub = [q_heads[h][u * ATTN_SUB:(u + 1) * ATTN_SUB, :] for u, h in chains]
    key_row = lax.broadcasted_iota(jnp.int32, (tq, ATTN_SUB), 0)
    q_col = lax.broadcasted_iota(jnp.int32, (tq, ATTN_SUB), 1)

    def step(blocks, stats):
        rows = [pl.ds(pl.multiple_of(j * tq, tq), tq) for j, _ in blocks]
        s = []
        for (_, diagonal), r in zip(blocks, rows):
            kb = kn_ref[r, :]
            bias = [jnp.concatenate([bias_ref[h, r, :]] * (ATTN_SUB // LANES), axis=1) for h in range(2)]
            sb = [_dot_nt(kb, q_sub[i]) + bias[h] for i, (u, h) in enumerate(chains)]
            if diagonal:
                sb = [jnp.where(key_row <= q_col + u * ATTN_SUB, x, NEG_BIG) for x, (u, h) in zip(sb, chains)]
            s.append(sb)
        m_new = [functools.reduce(jnp.maximum, [jnp.max(sb[i], axis=0, keepdims=True) for sb in s], st[0])
                 for i, st in enumerate(stats)]
        alpha = [jnp.exp2(st[0] - m) for st, m in zip(stats, m_new)]
        p = [[jnp.exp2(x - m) for x, m in zip(sb, m_new)] for sb in s]
        l_new = [a * st[1] + sum(jnp.sum(pb[i], axis=0, keepdims=True) for pb in p)
                 for i, (a, st) in enumerate(zip(alpha, stats))]
        pv = [sum(_dot(vt_ref[h * HEAD_DIM:(h + 1) * HEAD_DIM, r], pb[i].astype(BF16)) for pb, r in zip(p, rows))
              for i, (u, h) in enumerate(chains)]
        return tuple((m, l, a * st[2] + y) for m, l, a, st, y in zip(m_new, l_new, alpha, stats, pv))

    def finish(stats):
        for u in range(n_sub):
            (_, l0, a0), (_, l1, a1) = stats[2 * u], stats[2 * u + 1]
            attn = jnp.concatenate([a0 / l0, a1 / l1], axis=0).T
            rows = slice(u * ATTN_SUB, (u + 1) * ATTN_SUB)
            o_ref[rows, :] = (attn * jax.nn.sigmoid(og_ref[rows, :].astype(F32))).astype(BF16)

    init = tuple((jnp.full((1, ATTN_SUB), NEG_BIG, F32), jnp.zeros((1, ATTN_SUB), F32),
                  jnp.zeros((HEAD_DIM, ATTN_SUB), F32)) for _ in chains)
    stats = lax.fori_loop(0, qi // 2, lambda i, c: step([(2 * i, False), (2 * i + 1, False)], c), init)

    @pl.when(qi % 2 == 0)
    def _():
        finish(step([(qi, True)], stats))

    @pl.when(qi % 2 == 1)
    def _():
        finish(step([(qi - 1, False), (qi, True)], stats))


def _fox_attn(qkvg, cum, q_gain2, k_gain2, batch, seq, tq=512):
    n = qkvg.shape[0]
    tq = min(tq, seq)
    nq = seq // tq
    return pl.pallas_call(
        functools.partial(_fox_attn_kernel, tq=tq, seq=seq),
        grid=(batch, FOX_PAIRS, nq),
        in_specs=[pl.BlockSpec((tq, LANES), lambda b, p, i: (b * nq + i, p)),
                  pl.BlockSpec((seq, LANES), lambda b, p, i: (b, FOX_PAIRS + p)),
                  pl.BlockSpec((seq, LANES), lambda b, p, i: (b, 2 * FOX_PAIRS + p)),
                  pl.BlockSpec((tq, LANES), lambda b, p, i: (b * nq + i, 3 * FOX_PAIRS + p)),
                  pl.BlockSpec((seq, LANES), lambda b, p, i: (b, 0)),
                  _const_spec(q_gain2.shape), _const_spec(k_gain2.shape)],
        out_specs=pl.BlockSpec((tq, LANES), lambda b, p, i: (b * nq + i, p)),
        out_shape=jax.ShapeDtypeStruct((n, FOX_WIDTH), BF16),
        scratch_shapes=[pltpu.VMEM((seq, LANES), BF16), pltpu.VMEM((LANES, seq), BF16),
                        pltpu.VMEM((2, seq, LANES), F32)],
        compiler_params=_cparams("parallel", "parallel", "arbitrary"),
    )(qkvg, qkvg, qkvg, qkvg, cum, q_gain2, k_gain2)


def _split_bf16(t):
    hi = t.astype(BF16)
    lo = (t - hi.astype(F32)).astype(BF16)
    return hi, lo


def _pool_kernel(u_ref, halo_ref, pw_ref, ps_ref, o_ref, *, tiles_per_seq):
    i = pl.program_id(0)
    tm = u_ref.shape[0]
    tile_in_seq = i % tiles_per_seq
    u = u_ref[...]
    halo = halo_ref[...] * (tile_in_seq != 0).astype(F32)
    row = lax.broadcasted_iota(jnp.int32, (tm, tm), 0)
    col = lax.broadcasted_iota(jnp.int32, (tm, tm), 1)
    hrow = lax.broadcasted_iota(jnp.int32, (tm, POOL_HALO), 0)
    hcol = lax.broadcasted_iota(jnp.int32, (tm, POOL_HALO), 1)
    pos = tile_in_seq * tm + lax.broadcasted_iota(jnp.int32, (tm, 1), 0)
    for g, w in enumerate(POOL_WINDOWS):
        sl = slice(g * LANES, (g + 1) * LANES)
        band = jnp.where((col <= row) & (row - col < w), 1.0, 0.0).astype(BF16)
        hband = jnp.where(hrow + POOL_HALO - hcol < w, 1.0, 0.0).astype(BF16)
        wsum2 = (_dot(band, jnp.concatenate(_split_bf16(u[:, sl]), axis=1))
                 + _dot(hband, jnp.concatenate(_split_bf16(halo[:, sl]), axis=1)))
        wsum = wsum2[:, :LANES] + wsum2[:, LANES:]
        count = jnp.minimum(pos + 1, w).astype(F32)
        pooled = wsum / count - u[:, sl]
        o_ref[:, sl] = (_dot(pooled.astype(BF16), pw_ref[g]) * ps_ref[:, sl]).astype(BF16)


def _pool(u, pool_w, pool_scale, seq, tm=256):
    n = u.shape[0]
    tm = min(tm, seq)
    hb = tm // POOL_HALO
    return pl.pallas_call(
        functools.partial(_pool_kernel, tiles_per_seq=seq // tm),
        grid=(n // tm,),
        in_specs=[pl.BlockSpec((tm, POOL_WIDTH), lambda i: (i, 0)),
                  pl.BlockSpec((POOL_HALO, POOL_WIDTH), lambda i: (jnp.maximum(i * hb - 1, 0), 0)),
                  _const_spec(pool_w.shape), _const_spec(pool_scale.shape)],
        out_specs=pl.BlockSpec((tm, POOL_WIDTH), lambda i: (i, 0)),
        out_shape=jax.ShapeDtypeStruct((n, POOL_WIDTH), BF16),
        compiler_params=_cparams("parallel"),
    )(u, u, pool_w, pool_scale)


def _proj_ffn_kernel(*refs, n_mix):
    x_ref = refs[0]
    mix_refs = refs[1:1 + 2 * n_mix]
    gain_ref, wg_ref, wu_ref, wd_ref, o_ref = refs[1 + 2 * n_mix:]
    x = x_ref[...]
    for m in range(n_mix):
        x = x + _dot(mix_refs[2 * m][...], mix_refs[2 * m + 1][...])
    hb = _rms(x, gain_ref[...]).astype(BF16)
    gate = _dot(hb, wg_ref[...])
    up = _dot(hb, wu_ref[...])
    act = (gate * jax.nn.sigmoid(gate) * up).astype(BF16)
    o_ref[...] = x + _dot(act, wd_ref[...])


def _proj_ffn(x, mixes, gain, w_gate, w_up, w_down, tm=512):
    n = x.shape[0]
    tm = min(tm, n)
    args, specs = [x], [pl.BlockSpec((tm, D_MODEL), lambda i: (i, 0))]
    for y, w in mixes:
        args += [y, w]
        specs += [pl.BlockSpec((tm, y.shape[1]), lambda i: (i, 0)), _const_spec(w.shape)]
    consts = [gain, w_gate, w_up, w_down]
    return pl.pallas_call(
        functools.partial(_proj_ffn_kernel, n_mix=len(mixes)),
        grid=(n // tm,),
        in_specs=specs + [_const_spec(c.shape) for c in consts],
        out_specs=pl.BlockSpec((tm, D_MODEL), lambda i: (i, 0)),
        out_shape=jax.ShapeDtypeStruct((n, D_MODEL), F32),
        compiler_params=_cparams("parallel"),
    )(*args, *consts)


def _rwkv_proj_kernel(*refs, tiles_per_seq, has_vmix):
    if has_vmix:
        (x_ref, halo_ref, gain_ref, mu_ref, wr_ref, wk_ref, wv_ref, w0_ref, w1_ref, w2_ref,
         a0_ref, a1_ref, a2_ref, g1_ref, g2_ref, vf_ref, v0_ref, v1_ref, v2_ref,
         r_ref, lw_ref, k_ref, v_ref, a_ref, g_ref) = refs
    else:
        (x_ref, halo_ref, gain_ref, mu_ref, wr_ref, wk_ref, wv_ref, w0_ref, w1_ref, w2_ref,
         a0_ref, a1_ref, a2_ref, g1_ref, g2_ref,
         r_ref, lw_ref, k_ref, v_ref, a_ref, g_ref) = refs
    i = pl.program_id(0)
    tm = x_ref.shape[0]
    gain = gain_ref[...]
    h = _rms(x_ref[...], gain)
    h_last = _rms(halo_ref[...], gain)[halo_ref.shape[0] - 1:, :]
    h_last = h_last * (i % tiles_per_seq != 0).astype(F32)
    rows = lax.broadcasted_iota(jnp.int32, (tm, 1), 0)
    h_prev = jnp.where(rows == 0, h_last, pltpu.roll(h, 1, axis=0))
    xx = h_prev - h

    def mix(idx):
        return (h + xx * mu_ref[idx:idx + 1, :]).astype(BF16)

    xr, xw, xk, xv, xa, xg = (mix(idx) for idx in range(6))
    r_ref[...] = _dot(xr, wr_ref[...]).astype(BF16)
    k_ref[...] = _dot(xk, wk_ref[...]).astype(BF16)
    v = _dot(xv, wv_ref[...])
    wl = w0_ref[...] + _dot(jnp.tanh(_dot(xw, w1_ref[...])).astype(BF16), w2_ref[...])
    lw_ref[...] = -DECAY_SCALE * jax.nn.sigmoid(wl)
    a_ref[...] = jax.nn.sigmoid(
        a0_ref[...] + _dot(_dot(xa, a1_ref[...]).astype(BF16), a2_ref[...])).astype(BF16)
    g_ref[...] = _dot(jax.nn.sigmoid(_dot(xg, g1_ref[...])).astype(BF16), g2_ref[...]).astype(BF16)
    if has_vmix:
        gate = jax.nn.sigmoid(v0_ref[...] + _dot(_dot(xv, v1_ref[...]).astype(BF16), v2_ref[...]))
        v = v + (vf_ref[...].astype(F32) - v) * gate
    v_ref[...] = v.astype(BF16)


def _rwkv_proj(x, gain, p, v_first, seq, tm=256):
    n = x.shape[0]
    tm = min(tm, seq)
    has_vmix = v_first is not None
    tile = pl.BlockSpec((tm, D_MODEL), lambda i: (i, 0))
    halo = pl.BlockSpec((8, D_MODEL), lambda i: (jnp.maximum(i * (tm // 8) - 1, 0), 0))
    consts = [gain, p['mu'], p['w_r'], p['w_k'], p['w_v'], p['w0'], p['w1'], p['w2'],
              p['a0'], p['a1'], p['a2'], p['g1'], p['g2']]
    args = [x, x] + consts
    specs = [tile, halo] + [_const_spec(c.shape) for c in consts]
    if has_vmix:
        extra = [p['v0'], p['v1'], p['v2']]
        args += [v_first] + extra
        specs += [tile] + [_const_spec(c.shape) for c in extra]
    out_dtypes = [BF16, F32, BF16, BF16, BF16, BF16]
    return pl.pallas_call(
        functools.partial(_rwkv_proj_kernel, tiles_per_seq=seq // tm, has_vmix=has_vmix),
        grid=(n // tm,),
        in_specs=specs,
        out_specs=[tile] * 6,
        out_shape=[jax.ShapeDtypeStruct((n, D_MODEL), dt) for dt in out_dtypes],
        compiler_params=_cparams("parallel"),
    )(*args)


def _rwkv_scan_kernel(r_ref, lw_ref, k_ref, v_ref, a_ref, g_ref, kk_ref, ka_ref, rk_ref, lnw_ref, lnb_ref,
                      o_ref, state_ref):
    t = pl.program_id(2)

    @pl.when(t == 0)
    def _():
        state_ref[...] = jnp.zeros_like(state_ref)

    L = CHUNK
    n_chunks = r_ref.shape[0] // L
    n_pairs = r_ref.shape[1] // LANES
    items = [(c, p) for c in range(n_chunks) for p in range(n_pairs)]
    first = lax.broadcasted_iota(jnp.int32, (1, LANES), 1) < HEAD_DIM
    trow = lax.broadcasted_iota(jnp.int32, (L, LANES), 0)
    tcol = lax.broadcasted_iota(jnp.int32, (L, LANES), 1) & (HEAD_DIM - 1)
    strict = tcol < trow
    incl = tcol <= trow
    same16 = (trow >> 4) == (tcol >> 4)
    same32 = (trow >> 5) == (tcol >> 5)
    eye = jnp.where(trow == tcol, 1.0, 0.0)
    same_head = ((lax.broadcasted_iota(jnp.int32, (LANES, LANES), 0) >> 6)
                 == (lax.broadcasted_iota(jnp.int32, (LANES, LANES), 1) >> 6))
    tri = jnp.where(lax.broadcasted_iota(jnp.int32, (L, L), 1) <= lax.broadcasted_iota(jnp.int32, (L, L), 0),
                    1.0, 0.0).astype(BF16)

    def bf(z):
        return z.astype(BF16)

    def stack(zb):
        return jnp.concatenate([jnp.where(first, zb, 0), jnp.where(first, 0, zb)], axis=0)

    def wmm(x, y):
        return _dot(bf(x), stack(bf(y)))

    def head_sums(z):
        s0 = jnp.sum(jnp.where(first, z, 0.0), axis=-1, keepdims=True)
        s1 = jnp.sum(jnp.where(first, 0.0, z), axis=-1, keepdims=True)
        return jnp.where(first, s0, s1)

    def tile(ref, c, p):
        return ref[c * L:(c + 1) * L, p * LANES:(p + 1) * LANES]

    def load(ref):
        return [tile(ref, c, p).astype(F32) for c, p in items]

    def lane_param(ref):
        return [ref[:, p * LANES:(p + 1) * LANES] for _, p in items]

    r, lw, kraw, v, asig = load(r_ref), load(lw_ref), load(k_ref), load(v_ref), load(a_ref)
    kk = [x * w for x, w in zip(kraw, lane_param(kk_ref))]
    kk = [x / jnp.maximum(jnp.sqrt(head_sums(x * x)), 1e-12) for x in kk]
    k = [x * (1.0 + (s - 1.0) * w) for x, s, w in zip(kraw, asig, lane_param(ka_ref))]
    b = [x * s for x, s in zip(kk, asig)]
    lw_hi = [bf(x) for x in lw]
    lw_lo = [bf(x - h.astype(F32)) for x, h in zip(lw, lw_hi)]
    cum = [_dot(tri, jnp.concatenate([h, l], axis=1)) for h, l in zip(lw_hi, lw_lo)]
    cum = [x[:, :LANES] + x[:, LANES:] for x in cum]
    cum_end = [x[L - 1:L, :] for x in cum]
    e_neg = [jnp.exp(-x) for x in cum]
    e_end = [jnp.exp(ce - x) for x, ce in zip(cum, cum_end)]
    rt = [x * jnp.exp(c) for x, c in zip(r, cum)]
    at_b = [bf(-x * jnp.exp(c - w)) for x, c, w in zip(kk, cum, lw)]
    bt = [bf(x * e) for x, e in zip(b, e_neg)]
    kt = [bf(x * e) for x, e in zip(k, e_neg)]
    bh = [bf(x * e) for x, e in zip(b, e_end)]
    kh = [bf(x * e) for x, e in zip(k, e_end)]
    v_b = [bf(x) for x in v]
    aa = [_dot_nt(jnp.concatenate([xa, bf(xr)], axis=0), jnp.concatenate([stack(xb), stack(xk)], axis=0))
          for xa, xr, xb, xk in zip(at_b, rt, bt, kt)]
    a_ab = [jnp.where(strict, x[:L, :LANES], 0.0) for x in aa]
    a_ak = [jnp.where(strict, x[:L, LANES:], 0.0) for x in aa]
    a_rb = [jnp.where(incl, x[L:, :LANES], 0.0) for x in aa]
    a_rk = [jnp.where(incl, x[L:, LANES:], 0.0) for x in aa]
    pw = [jnp.where(same16, x, 0.0) for x in a_ab]
    tinv = [eye + x for x in pw]
    for _ in range(3):
        pw = [wmm(x, x) for x in pw]
        tinv = [x + wmm(x, p) for x, p in zip(tinv, pw)]
    off = [jnp.where(same32 & ~same16, x, 0.0) for x in a_ab]
    tinv = [x + wmm(wmm(x, o), x) for x, o in zip(tinv, off)]
    off = [jnp.where(same32, 0.0, x) for x in a_ab]
    tinv = [x + wmm(wmm(x, o), x) for x, o in zip(tinv, off)]
    akv = [wmm(x, y) for x, y in zip(a_ak, v)]
    x12 = [_dot(bf(x), jnp.concatenate([stack(y), stack(bf(z))], axis=1))
           for x, y, z in zip(tinv, at_b, akv)]
    x1 = [bf(x[:, :LANES]) for x in x12]
    x2 = [bf(x[:, LANES:]) for x in x12]
    ab12 = [_dot(bf(x), jnp.concatenate([stack(y), stack(z)], axis=1)) for x, y, z in zip(a_rb, x1, x2)]
    y1 = [bf(x + y[:, :LANES]) for x, y in zip(rt, ab12)]
    y2 = [y[:, LANES:] + wmm(x, z) for y, x, z in zip(ab12, a_rk, v)]
    m_low = [bf(jnp.where(same_head, _dot_tn(x, y), 0.0)) for x, y in zip(x1, bh)]
    c_full = [_dot_tn(jnp.concatenate([x, y], axis=0), jnp.concatenate([z, w], axis=0))
              for x, y, z, w in zip(x2, v_b, bh, kh)]
    c_wide = [jnp.where(first, x[:L], x[L:]) for x in c_full]
    p_end = [jnp.exp(x) for x in cum_end]

    states = [state_ref[:, p * LANES:(p + 1) * LANES] for p in range(n_pairs)]
    y = []
    for i, (c, p) in enumerate(items):
        state_b = bf(states[p])
        y.append(_dot_nt(y1[i], stack(state_b)) + y2[i])
        states[p] = states[p] * p_end[i] + _dot(state_b, m_low[i]) + c_wide[i]
    for p in range(n_pairs):
        state_ref[:, p * LANES:(p + 1) * LANES] = states[p]

    lnw, lnb, rk = lane_param(lnw_ref), lane_param(lnb_ref), lane_param(rk_ref)
    for i, (c, p) in enumerate(items):
        mean = head_sums(y[i]) / HEAD_DIM
        dev = y[i] - mean
        var = head_sums(dev * dev) / HEAD_DIM
        z = dev * lax.rsqrt(var + GN_EPS) * lnw[i] + lnb[i]
        bonus = head_sums(r[i] * k[i] * rk[i]) * v[i]
        o_ref[c * L:(c + 1) * L, p * LANES:(p + 1) * LANES] = (
            (z + bonus) * tile(g_ref, c, p).astype(F32)).astype(BF16)


def _rwkv_scan(r, lw, k, v, a, g, p, batch, seq, tb=256, lanes=4 * LANES):
    n = r.shape[0]
    tb = min(tb, seq)
    nt = seq // tb
    tile = pl.BlockSpec((tb, lanes), lambda b, h, t: (b * nt + t, h))
    lane_const = pl.BlockSpec((1, lanes), lambda b, h, t: (0, h))
    return pl.pallas_call(
        _rwkv_scan_kernel,
        grid=(batch, D_MODEL // lanes, nt),
        in_specs=[tile] * 6 + [lane_const] * 5,
        out_specs=tile,
        out_shape=jax.ShapeDtypeStruct((n, D_MODEL), BF16),
        scratch_shapes=[pltpu.VMEM((HEAD_DIM, lanes), F32)],
        compiler_params=_cparams("parallel", "parallel", "arbitrary"),
    )(r, lw, k, v, a, g, p['k_k'], p['k_a'], p['r_k'], p['ln_w'], p['ln_b'])


def _pad_to(t, axis, size):
    pad = [(0, 0)] * t.ndim
    pad[axis] = (0, size - t.shape[axis])
    return jnp.pad(t, pad)


def _row(t):
    return t.reshape(1, -1).astype(F32)


def _pair_tile(gain):
    return jnp.tile(gain.astype(F32), 2).reshape(1, LANES)


def kernel(x, mix_norm, ffn_norm, ffn_w_gate, ffn_w_up, ffn_w_down, hy_w_in, hy_f_bias, hy_q_gain, hy_k_gain, hy_pool_w, hy_pool_scale, hy_w_out, rw_mu, rw_w_r, rw_w_k, rw_w_v, rw_w0, rw_w1, rw_w2, rw_a0, rw_a1, rw_a2, rw_g1, rw_g2, rw_k_k, rw_k_a, rw_r_k, rw_ln_w, rw_ln_b, rw_w_o, rw_v0, rw_v1, rw_v2):
    batch, seq, d_model = x.shape
    assert d_model == D_MODEL and seq % 256 == 0
    depth = mix_norm.shape[0]
    n = batch * seq
    xs = x.reshape(n, d_model)
    v_first = None
    for layer in range(depth):
        gain = _row(mix_norm[layer])
        if layer % 2 == 0:
            e = layer // 2
            w_in = hy_w_in[e]
            w_main = jnp.concatenate([w_in[:, :4 * FOX_WIDTH], w_in[:, 4 * FOX_WIDTH + FOX_HEADS:]], axis=1).astype(BF16)
            wf = _pad_to(w_in[:, 4 * FOX_WIDTH:4 * FOX_WIDTH + FOX_HEADS], 1, LANES).astype(BF16)
            f_bias = _pad_to(_row(hy_f_bias[e]), 1, LANES)
            qkvg, u, cum = _inproj_even(xs, gain, w_main, wf, f_bias, seq)
            attn = _fox_attn(qkvg, cum, _pair_tile(hy_q_gain[e]), _pair_tile(hy_k_gain[e]), batch, seq)
            mixed = _pool(u, hy_pool_w[e].astype(BF16), _row(hy_pool_scale[e]), seq)
            w_out = hy_w_out[e].astype(BF16)
            mixes = [(attn, w_out[:FOX_WIDTH]), (mixed, w_out[FOX_WIDTH:])]
        else:
            o = layer // 2
            p = dict(mu=rw_mu[o].astype(F32),
                     w_r=rw_w_r[o].astype(BF16), w_k=rw_w_k[o].astype(BF16), w_v=rw_w_v[o].astype(BF16),
                     w0=_row(rw_w0[o]), w1=_pad_to(rw_w1[o], 1, LANES).astype(BF16),
                     w2=_pad_to(rw_w2[o], 0, LANES).astype(BF16),
                     a0=_row(rw_a0[o]), a1=_pad_to(rw_a1[o], 1, LANES).astype(BF16),
                     a2=_pad_to(rw_a2[o], 0, LANES).astype(BF16),
                     g1=_pad_to(rw_g1[o], 1, 2 * LANES).astype(BF16),
                     g2=_pad_to(rw_g2[o], 0, 2 * LANES).astype(BF16),
                     k_k=_row(rw_k_k[o]), k_a=_row(rw_k_a[o]), r_k=_row(rw_r_k[o]),
                     ln_w=_row(rw_ln_w[o]), ln_b=_row(rw_ln_b[o]))
            if o > 0:
                p.update(v0=_row(rw_v0[o - 1]), v1=_pad_to(rw_v1[o - 1], 1, LANES).astype(BF16),
                         v2=_pad_to(rw_v2[o - 1], 0, LANES).astype(BF16))
            r, lw, k, v, a, g = _rwkv_proj(xs, gain, p, v_first if o > 0 else None, seq)
            if o == 0:
                v_first = v
            mixes = [(_rwkv_scan(r, lw, k, v, a, g, p, batch, seq), rw_w_o[o].astype(BF16))]
        xs = _proj_ffn(xs, mixes, _row(ffn_norm[layer]), ffn_w_gate[layer].astype(BF16),
                       ffn_w_up[layer].astype(BF16), ffn_w_down[layer].astype(BF16))
    return xs.reshape(batch, seq, d_model)
```

```python
import functools

import jax
import jax.numpy as jnp
from jax import lax
from jax.experimental import pallas as pl
from jax.experimental.pallas import tpu as pltpu

F32 = jnp.float32
BF16 = jnp.bfloat16

D_MODEL = 1024
HEAD_DIM = 64
LANES = 128
FOX_HEADS = 8
FOX_WIDTH = FOX_HEADS * HEAD_DIM
FOX_PAIRS = FOX_HEADS // 2
POOL_WINDOWS = (2, 4, 8, 16)
POOL_WIDTH = D_MODEL - FOX_WIDTH
POOL_HALO = 16
RMS_EPS = 1e-6
GN_EPS = 64e-5
CHUNK = 64
DECAY_SCALE = 0.6065306597126334
NEG_BIG = -1e30
LOG2E = 1.4426950408889634
ATTN_SUB = 512
VMEM_LIMIT = 56 * 1024 * 1024


def _cparams(*sem, flags=None):
    return pltpu.CompilerParams(dimension_semantics=sem, vmem_limit_bytes=VMEM_LIMIT, flags=flags)


def _rms(x, gain):
    ms = jnp.mean(x * x, axis=-1, keepdims=True)
    return x * lax.rsqrt(ms + RMS_EPS) * gain


def _dot(a, b):
    return jnp.dot(a, b, preferred_element_type=F32)


def _dot_nt(a, b):
    return lax.dot_general(a, b, (((1,), (1,)), ((), ())), preferred_element_type=F32)


def _dot_tn(a, b):
    return lax.dot_general(a, b, (((0,), (0,)), ((), ())), preferred_element_type=F32)


def _const_spec(shape):
    nd = len(shape)
    return pl.BlockSpec(shape, lambda *_: (0,) * nd, pipeline_mode=pl.Buffered(1))


def _pair_rms_scale(t, first):
    t2 = t * t
    s0 = jnp.sum(jnp.where(first, t2, 0.0), axis=-1, keepdims=True)
    s1 = jnp.sum(jnp.where(first, 0.0, t2), axis=-1, keepdims=True)
    return jnp.where(first, lax.rsqrt(s0 / HEAD_DIM + RMS_EPS), lax.rsqrt(s1 / HEAD_DIM + RMS_EPS))


def _split3_bf16(t):
    hi = t.astype(BF16)
    r = t - hi.astype(F32)
    mid = r.astype(BF16)
    return hi, mid, (r - mid.astype(F32)).astype(BF16)


def _inproj_even_kernel(x_ref, gain_ref, wk_ref, wqg_ref, wu_ref, wvt_ref, wf_ref, fb_ref, kg_ref, tri_ref,
                        qg_ref, kaug_ref, vt_ref, u_ref, carry_ref, *, tiles_per_seq):
    i = pl.program_id(0)
    hb = _rms(x_ref[...], gain_ref[...]).astype(BF16)
    z = _dot(hb, wf_ref[...]) + fb_ref[...]
    c = jnp.minimum(z, 0.0) - jnp.log1p(jnp.exp(-jnp.abs(z)))
    tm = c.shape[0]
    c3 = _dot(tri_ref[...], jnp.concatenate(_split3_bf16(c), axis=1))
    c = c3[:, :LANES] + c3[:, LANES:2 * LANES] + c3[:, 2 * LANES:]

    @pl.when(i % tiles_per_seq == 0)
    def _():
        carry_ref[...] = jnp.zeros_like(carry_ref)

    c = c + carry_ref[...]
    carry_ref[...] = c[tm - 1:tm, :]
    main = _dot(hb, wk_ref[...])
    lane = lax.broadcasted_iota(jnp.int32, (1, LANES), 1)
    first = lane < HEAD_DIM
    key_bias = c * (-LOG2E)
    for p in range(FOX_PAIRS):
        k = main[:, p * LANES:(p + 1) * LANES]
        kn = k * _pair_rms_scale(k, first) * kg_ref[...]
        for h in range(2):
            hi, mid, lo = (t.astype(F32) for t in _split3_bf16(key_bias[:, 2 * p + h:2 * p + h + 1]))
            spare = HEAD_DIM * (1 - h)
            pieces = jnp.where(lane == spare, hi, jnp.where(lane == spare + 1, mid,
                                                            jnp.where(lane == spare + 2, lo, 0.0)))
            own = first if h == 0 else jnp.logical_not(first)
            kaug_ref[:, (2 * p + h) * LANES:(2 * p + h + 1) * LANES] = jnp.where(own, kn, pieces).astype(BF16)
    qg_ref[...] = _dot(hb, wqg_ref[...]).astype(BF16)
    u_ref[...] = _dot(hb, wu_ref[...])
    vt_ref[...] = _dot_nt(wvt_ref[...], hb).astype(BF16)


def _inproj_even(x, gain, w_k, w_qg, w_u, wv_t, wf, f_bias, k_gain2, seq, tm=512):
    n = x.shape[0]
    tm = min(tm, seq)
    tri = jnp.tril(jnp.ones((tm, tm), BF16))
    consts = [gain, w_k, w_qg, w_u, wv_t, wf, f_bias, k_gain2, tri]
    return pl.pallas_call(
        functools.partial(_inproj_even_kernel, tiles_per_seq=seq // tm),
        grid=(n // tm,),
        in_specs=[pl.BlockSpec((tm, D_MODEL), lambda i: (i, 0))] + [_const_spec(c.shape) for c in consts],
        out_specs=[pl.BlockSpec((tm, 2 * FOX_WIDTH), lambda i: (i, 0)),
                   pl.BlockSpec((tm, 2 * FOX_WIDTH), lambda i: (i, 0)),
                   pl.BlockSpec((FOX_WIDTH, tm), lambda i: (0, i)),
                   pl.BlockSpec((tm, POOL_WIDTH), lambda i: (i, 0))],
        out_shape=[jax.ShapeDtypeStruct((n, 2 * FOX_WIDTH), BF16),
                   jax.ShapeDtypeStruct((n, 2 * FOX_WIDTH), BF16),
                   jax.ShapeDtypeStruct((FOX_WIDTH, n), BF16),
                   jax.ShapeDtypeStruct((n, POOL_WIDTH), F32)],
        scratch_shapes=[pltpu.VMEM((1, LANES), F32)],
        compiler_params=_cparams("arbitrary"),
    )(x, *consts)


def _fox_attn_kernel(q_ref, og_ref, k0_ref, k1_ref, vt_ref, qg_ref, o_ref, *, tq):
    qi = pl.program_id(2)
    lane = lax.broadcasted_iota(jnp.int32, (1, LANES), 1)
    first = lane < HEAD_DIM
    k_refs = (k0_ref, k1_ref)

    q = q_ref[...].astype(F32)
    qn = q * _pair_rms_scale(q, first) * (qg_ref[...] * (HEAD_DIM ** -0.5 * LOG2E))
    ones0 = jnp.where((lane >= HEAD_DIM) & (lane < HEAD_DIM + 3), 1.0, 0.0)
    ones1 = jnp.where(lane < 3, 1.0, 0.0)
    q_heads = (jnp.where(first, qn, ones0).astype(BF16), jnp.where(first, ones1, qn).astype(BF16))
    n_sub = tq // ATTN_SUB
    chains = [(u, h) for u in range(n_sub) for h in range(2)]
    q_sub = [q_heads[h][u * ATTN_SUB:(u + 1) * ATTN_SUB, :] for u, h in chains]
    key_row = lax.broadcasted_iota(jnp.int32, (tq, ATTN_SUB), 0)
    q_col = lax.broadcasted_iota(jnp.int32, (tq, ATTN_SUB), 1)

    def step(blocks, stats):
        rows = [pl.ds(pl.multiple_of(j * tq, tq), tq) for j, _ in blocks]
        s = []
        for (_, diagonal), r in zip(blocks, rows):
            sb = [_dot_nt(k_refs[h][r, :], q_sub[i]) for i, (u, h) in enumerate(chains)]
            if diagonal:
                sb = [jnp.where(key_row <= q_col + u * ATTN_SUB, x, NEG_BIG) for x, (u, h) in zip(sb, chains)]
            s.append(sb)
        m_new = [functools.reduce(jnp.maximum, [jnp.max(sb[i], axis=0, keepdims=True) for sb in s], st[0])
                 for i, st in enumerate(stats)]
        alpha = [jnp.exp2(st[0] - m) for st, m in zip(stats, m_new)]
        p = [[jnp.exp2(x - m) for x, m in zip(sb, m_new)] for sb in s]
        l_new = [a * st[1] + sum(jnp.sum(pb[i], axis=0, keepdims=True) for pb in p)
                 for i, (a, st) in enumerate(zip(alpha, stats))]
        pv = [sum(_dot(vt_ref[h * HEAD_DIM:(h + 1) * HEAD_DIM, r], pb[i].astype(BF16)) for pb, r in zip(p, rows))
              for i, (u, h) in enumerate(chains)]
        return tuple((m, l, a * st[2] + y) for m, l, a, st, y in zip(m_new, l_new, alpha, stats, pv))

    def finish(stats):
        for u in range(n_sub):
            (_, l0, a0), (_, l1, a1) = stats[2 * u], stats[2 * u + 1]
            attn = jnp.concatenate([a0 / l0, a1 / l1], axis=0).T
            rows = slice(u * ATTN_SUB, (u + 1) * ATTN_SUB)
            o_ref[rows, :] = (attn * jax.nn.sigmoid(og_ref[rows, :].astype(F32))).astype(BF16)

    init = tuple((jnp.full((1, ATTN_SUB), NEG_BIG, F32), jnp.zeros((1, ATTN_SUB), F32),
                  jnp.zeros((HEAD_DIM, ATTN_SUB), F32)) for _ in chains)
    stats = lax.fori_loop(0, qi // 2, lambda i, c: step([(2 * i, False), (2 * i + 1, False)], c), init)

    @pl.when(qi % 2 == 0)
    def _():
        finish(step([(qi, True)], stats))

    @pl.when(qi % 2 == 1)
    def _():
        finish(step([(qi - 1, False), (qi, True)], stats))


def _fox_attn(qg, kaug, vt, q_gain2, batch, seq, tq=512):
    n = qg.shape[0]
    tq = min(tq, seq)
    nq = seq // tq
    return pl.pallas_call(
        functools.partial(_fox_attn_kernel, tq=tq),
        grid=(batch, FOX_PAIRS, nq),
        in_specs=[pl.BlockSpec((tq, LANES), lambda b, p, i: (b * nq + i, p)),
                  pl.BlockSpec((tq, LANES), lambda b, p, i: (b * nq + i, FOX_PAIRS + p)),
                  pl.BlockSpec((seq, LANES), lambda b, p, i: (b, 2 * p)),
                  pl.BlockSpec((seq, LANES), lambda b, p, i: (b, 2 * p + 1)),
                  pl.BlockSpec((LANES, seq), lambda b, p, i: (p, b)),
                  _const_spec(q_gain2.shape)],
        out_specs=pl.BlockSpec((tq, LANES), lambda b, p, i: (b * nq + i, p)),
        out_shape=jax.ShapeDtypeStruct((n, FOX_WIDTH), BF16),
        compiler_params=_cparams("parallel", "parallel", "arbitrary"),
    )(qg, qg, kaug, kaug, vt, q_gain2)


def _split_bf16(t):
    hi = t.astype(BF16)
    lo = (t - hi.astype(F32)).astype(BF16)
    return hi, lo


def _pool_kernel(u_ref, halo_ref, pw_ref, ps_ref, o_ref, *, tiles_per_seq):
    i = pl.program_id(0)
    tm = u_ref.shape[0]
    tile_in_seq = i % tiles_per_seq
    u = u_ref[...]
    halo = halo_ref[...] * (tile_in_seq != 0).astype(F32)
    row = lax.broadcasted_iota(jnp.int32, (tm, tm), 0)
    col = lax.broadcasted_iota(jnp.int32, (tm, tm), 1)
    hrow = lax.broadcasted_iota(jnp.int32, (tm, POOL_HALO), 0)
    hcol = lax.broadcasted_iota(jnp.int32, (tm, POOL_HALO), 1)
    pos = tile_in_seq * tm + lax.broadcasted_iota(jnp.int32, (tm, 1), 0)
    groups = range(len(POOL_WINDOWS))
    sl = [slice(g * LANES, (g + 1) * LANES) for g in groups]
    band = [jnp.where((col <= row) & (row - col < w), 1.0, 0.0).astype(BF16) for w in POOL_WINDOWS]
    hband = [jnp.where(hrow + POOL_HALO - hcol < w, 1.0, 0.0).astype(BF16) for w in POOL_WINDOWS]
    wsum2 = [_dot(band[g], jnp.concatenate(_split_bf16(u[:, sl[g]]), axis=1))
             + _dot(hband[g], jnp.concatenate(_split_bf16(halo[:, sl[g]]), axis=1)) for g in groups]
    pooled = [(wsum2[g][:, :LANES] + wsum2[g][:, LANES:]) / jnp.minimum(pos + 1, w).astype(F32) - u[:, sl[g]]
              for g, w in enumerate(POOL_WINDOWS)]
    mixed = [_dot(pooled[g].astype(BF16), pw_ref[g]) for g in groups]
    for g in groups:
        o_ref[:, sl[g]] = (mixed[g] * ps_ref[:, sl[g]]).astype(BF16)


def _pool(u, pool_w, pool_scale, seq, tm=512):
    n = u.shape[0]
    tm = min(tm, seq)
    hb = tm // POOL_HALO
    return pl.pallas_call(
        functools.partial(_pool_kernel, tiles_per_seq=seq // tm),
        grid=(n // tm,),
        in_specs=[pl.BlockSpec((tm, POOL_WIDTH), lambda i: (i, 0)),
                  pl.BlockSpec((POOL_HALO, POOL_WIDTH), lambda i: (jnp.maximum(i * hb - 1, 0), 0)),
                  _const_spec(pool_w.shape), _const_spec(pool_scale.shape)],
        out_specs=pl.BlockSpec((tm, POOL_WIDTH), lambda i: (i, 0)),
        out_shape=jax.ShapeDtypeStruct((n, POOL_WIDTH), BF16),
        compiler_params=_cparams("parallel"),
    )(u, u, pool_w, pool_scale)


def _proj_ffn_kernel(*refs, n_mix):
    x_ref = refs[0]
    mix_refs = refs[1:1 + 2 * n_mix]
    gain_ref, wg_ref, wu_ref, wd_ref, o_ref = refs[1 + 2 * n_mix:]
    x = x_ref[...]
    for m in range(n_mix):
        x = x + _dot(mix_refs[2 * m][...], mix_refs[2 * m + 1][...])
    hb = _rms(x, gain_ref[...]).astype(BF16)
    gate = _dot(hb, wg_ref[...])
    up = _dot(hb, wu_ref[...])
    act = (gate * jax.nn.sigmoid(gate) * up).astype(BF16)
    o_ref[...] = x + _dot(act, wd_ref[...])


def _proj_ffn(x, mixes, gain, w_gate, w_up, w_down, tm=512):
    n = x.shape[0]
    tm = min(tm, n)
    args, specs = [x], [pl.BlockSpec((tm, D_MODEL), lambda i: (i, 0))]
    for y, w in mixes:
        args += [y, w]
        specs += [pl.BlockSpec((tm, y.shape[1]), lambda i: (i, 0)), _const_spec(w.shape)]
    consts = [gain, w_gate, w_up, w_down]
    return pl.pallas_call(
        functools.partial(_proj_ffn_kernel, n_mix=len(mixes)),
        grid=(n // tm,),
        in_specs=specs + [_const_spec(c.shape) for c in consts],
        out_specs=pl.BlockSpec((tm, D_MODEL), lambda i: (i, 0)),
        out_shape=jax.ShapeDtypeStruct((n, D_MODEL), F32),
        compiler_params=_cparams("parallel"),
    )(*args, *consts)


def _rwkv_proj_kernel(*refs, tiles_per_seq, has_vmix):
    if has_vmix:
        (x_ref, halo_ref, gain_ref, mu_ref, wr_ref, wk_ref, wv_ref, w0_ref, w1_ref, w2_ref,
         a0_ref, a1_ref, a2_ref, g1_ref, g2_ref, vf_ref, v0_ref, v1_ref, v2_ref,
         r_ref, lw_ref, k_ref, v_ref, a_ref, g_ref) = refs
    else:
        (x_ref, halo_ref, gain_ref, mu_ref, wr_ref, wk_ref, wv_ref, w0_ref, w1_ref, w2_ref,
         a0_ref, a1_ref, a2_ref, g1_ref, g2_ref,
         r_ref, lw_ref, k_ref, v_ref, a_ref, g_ref) = refs
    i = pl.program_id(0)
    tm = x_ref.shape[0]
    gain = gain_ref[...]
    h = _rms(x_ref[...], gain)
    h_last = _rms(halo_ref[...], gain)[halo_ref.shape[0] - 1:, :]
    h_last = h_last * (i % tiles_per_seq != 0).astype(F32)
    rows = lax.broadcasted_iota(jnp.int32, (tm, 1), 0)
    h_prev = jnp.where(rows == 0, h_last, pltpu.roll(h, 1, axis=0))
    xx = h_prev - h

    def mix(idx):
        return (h + xx * mu_ref[idx:idx + 1, :]).astype(BF16)

    xr, xw, xk, xv, xa, xg = (mix(idx) for idx in range(6))
    r_ref[...] = _dot(xr, wr_ref[...]).astype(BF16)
    k_ref[...] = _dot(xk, wk_ref[...]).astype(BF16)
    v = _dot(xv, wv_ref[...])
    wl = w0_ref[...] + _dot(jnp.tanh(_dot(xw, w1_ref[...])).astype(BF16), w2_ref[...])
    lw_ref[...] = -DECAY_SCALE * jax.nn.sigmoid(wl)
    a_ref[...] = jax.nn.sigmoid(
        a0_ref[...] + _dot(_dot(xa, a1_ref[...]).astype(BF16), a2_ref[...])).astype(BF16)
    g_ref[...] = _dot(jax.nn.sigmoid(_dot(xg, g1_ref[...])).astype(BF16), g2_ref[...]).astype(BF16)
    if has_vmix:
        gate = jax.nn.sigmoid(v0_ref[...] + _dot(_dot(xv, v1_ref[...]).astype(BF16), v2_ref[...]))
        v = v + (vf_ref[...].astype(F32) - v) * gate
    v_ref[...] = v.astype(BF16)


def _rwkv_proj(x, gain, p, v_first, seq, tm=256):
    n = x.shape[0]
    tm = min(tm, seq)
    has_vmix = v_first is not None
    tile = pl.BlockSpec((tm, D_MODEL), lambda i: (i, 0))
    halo = pl.BlockSpec((8, D_MODEL), lambda i: (jnp.maximum(i * (tm // 8) - 1, 0), 0))
    consts = [gain, p['mu'], p['w_r'], p['w_k'], p['w_v'], p['w0'], p['w1'], p['w2'],
              p['a0'], p['a1'], p['a2'], p['g1'], p['g2']]
    args = [x, x] + consts
    specs = [tile, halo] + [_const_spec(c.shape) for c in consts]
    if has_vmix:
        extra = [p['v0'], p['v1'], p['v2']]
        args += [v_first] + extra
        specs += [tile] + [_const_spec(c.shape) for c in extra]
    out_dtypes = [BF16, F32, BF16, BF16, BF16, BF16]
    return pl.pallas_call(
        functools.partial(_rwkv_proj_kernel, tiles_per_seq=seq // tm, has_vmix=has_vmix),
        grid=(n // tm,),
        in_specs=specs,
        out_specs=[tile] * 6,
        out_shape=[jax.ShapeDtypeStruct((n, D_MODEL), dt) for dt in out_dtypes],
        compiler_params=_cparams("parallel"),
    )(*args)


def _rwkv_scan_kernel(r_ref, lw_ref, k_ref, v_ref, a_ref, g_ref, kk_ref, ka_ref, rk_ref, lnw_ref, lnb_ref,
                      o_ref, state_ref):
    t = pl.program_id(2)

    @pl.when(t == 0)
    def _():
        state_ref[...] = jnp.zeros_like(state_ref)

    L = CHUNK
    n_chunks = r_ref.shape[0] // L
    n_pairs = r_ref.shape[1] // LANES
    items = [(c, p) for c in range(n_chunks) for p in range(n_pairs)]
    first = lax.broadcasted_iota(jnp.int32, (1, LANES), 1) < HEAD_DIM
    trow = lax.broadcasted_iota(jnp.int32, (L, LANES), 0)
    tcol = lax.broadcasted_iota(jnp.int32, (L, LANES), 1) & (HEAD_DIM - 1)
    strict = tcol < trow
    incl = tcol <= trow
    same16 = (trow >> 4) == (tcol >> 4)
    same32 = (trow >> 5) == (tcol >> 5)
    eye = jnp.where(trow == tcol, 1.0, 0.0)
    same_head = ((lax.broadcasted_iota(jnp.int32, (LANES, LANES), 0) >> 6)
                 == (lax.broadcasted_iota(jnp.int32, (LANES, LANES), 1) >> 6))
    tri = jnp.where(lax.broadcasted_iota(jnp.int32, (L, L), 1) <= lax.broadcasted_iota(jnp.int32, (L, L), 0),
                    1.0, 0.0).astype(BF16)

    def bf(z):
        return z.astype(BF16)

    def stack(zb):
        return jnp.concatenate([jnp.where(first, zb, 0), jnp.where(first, 0, zb)], axis=0)

    def wmm(x, y):
        return _dot(bf(x), stack(bf(y)))

    def head_sums(z):
        s0 = jnp.sum(jnp.where(first, z, 0.0), axis=-1, keepdims=True)
        s1 = jnp.sum(jnp.where(first, 0.0, z), axis=-1, keepdims=True)
        return jnp.where(first, s0, s1)

    def tile(ref, c, p):
        return ref[c * L:(c + 1) * L, p * LANES:(p + 1) * LANES]

    def load(ref):
        return [tile(ref, c, p).astype(F32) for c, p in items]

    def lane_param(ref):
        return [ref[:, p * LANES:(p + 1) * LANES] for _, p in items]

    r, lw, kraw, v, asig = load(r_ref), load(lw_ref), load(k_ref), load(v_ref), load(a_ref)
    kk = [x * w for x, w in zip(kraw, lane_param(kk_ref))]
    kk = [x / jnp.maximum(jnp.sqrt(head_sums(x * x)), 1e-12) for x in kk]
    k = [x * (1.0 + (s - 1.0) * w) for x, s, w in zip(kraw, asig, lane_param(ka_ref))]
    b = [x * s for x, s in zip(kk, asig)]
    lw_hi = [bf(x) for x in lw]
    lw_lo = [bf(x - h.astype(F32)) for x, h in zip(lw, lw_hi)]
    cum = [_dot(tri, jnp.concatenate([h, l], axis=1)) for h, l in zip(lw_hi, lw_lo)]
    cum = [x[:, :LANES] + x[:, LANES:] for x in cum]
    cum_end = [x[L - 1:L, :] for x in cum]
    e_neg = [jnp.exp(-x) for x in cum]
    e_end = [jnp.exp(ce - x) for x, ce in zip(cum, cum_end)]
    rt = [x * jnp.exp(c) for x, c in zip(r, cum)]
    at_b = [bf(-x * jnp.exp(c - w)) for x, c, w in zip(kk, cum, lw)]
    bt = [bf(x * e) for x, e in zip(b, e_neg)]
    kt = [bf(x * e) for x, e in zip(k, e_neg)]
    bh = [bf(x * e) for x, e in zip(b, e_end)]
    kh = [bf(x * e) for x, e in zip(k, e_end)]
    v_b = [bf(x) for x in v]
    aa = [_dot_nt(jnp.concatenate([xa, bf(xr)], axis=0), jnp.concatenate([stack(xb), stack(xk)], axis=0))
          for xa, xr, xb, xk in zip(at_b, rt, bt, kt)]
    a_ab = [jnp.where(strict, x[:L, :LANES], 0.0) for x in aa]
    a_ak = [jnp.where(strict, x[:L, LANES:], 0.0) for x in aa]
    a_rb = [jnp.where(incl, x[L:, :LANES], 0.0) for x in aa]
    a_rk = [jnp.where(incl, x[L:, LANES:], 0.0) for x in aa]
    pw = [jnp.where(same16, x, 0.0) for x in a_ab]
    tinv = [eye + x for x in pw]
    pw = [wmm(x, x) for x in pw]
    for _ in range(2):
        both = [_dot(jnp.concatenate([bf(x), bf(p)], axis=0), stack(bf(p))) for x, p in zip(tinv, pw)]
        tinv = [x + y[:L] for x, y in zip(tinv, both)]
        pw = [y[L:] for y in both]
    tinv = [x + wmm(x, p) for x, p in zip(tinv, pw)]
    off = [jnp.where(same32 & ~same16, x, 0.0) for x in a_ab]
    tinv = [x + wmm(wmm(x, o), x) for x, o in zip(tinv, off)]
    off = [jnp.where(same32, 0.0, x) for x in a_ab]
    tinv = [x + wmm(wmm(x, o), x) for x, o in zip(tinv, off)]
    akv = [wmm(x, y) for x, y in zip(a_ak, v)]
    x12 = [_dot(bf(x), jnp.concatenate([stack(y), stack(bf(z))], axis=1))
           for x, y, z in zip(tinv, at_b, akv)]
    x1 = [bf(x[:, :LANES]) for x in x12]
    x2 = [bf(x[:, LANES:]) for x in x12]
    ab12 = [_dot(bf(x), jnp.concatenate([stack(y), stack(z)], axis=1)) for x, y, z in zip(a_rb, x1, x2)]
    y1 = [bf(x + y[:, :LANES]) for x, y in zip(rt, ab12)]
    y2 = [y[:, LANES:] + wmm(x, z) for y, x, z in zip(ab12, a_rk, v)]
    m_low = [bf(jnp.where(same_head, _dot_tn(x, y), 0.0)) for x, y in zip(x1, bh)]
    c_full = [_dot_tn(jnp.concatenate([x, y], axis=0), jnp.concatenate([z, w], axis=0))
              for x, y, z, w in zip(x2, v_b, bh, kh)]
    c_wide = [jnp.where(first, x[:L], x[L:]) for x in c_full]
    p_end = [jnp.exp(x) for x in cum_end]

    states = [state_ref[:, p * LANES:(p + 1) * LANES] for p in range(n_pairs)]
    y = []
    for i, (c, p) in enumerate(items):
        state_b = bf(states[p])
        y.append(_dot_nt(y1[i], stack(state_b)) + y2[i])
        states[p] = states[p] * p_end[i] + _dot(state_b, m_low[i]) + c_wide[i]
    for p in range(n_pairs):
        state_ref[:, p * LANES:(p + 1) * LANES] = states[p]

    lnw, lnb, rk = lane_param(lnw_ref), lane_param(lnb_ref), lane_param(rk_ref)
    for i, (c, p) in enumerate(items):
        mean = head_sums(y[i]) / HEAD_DIM
        dev = y[i] - mean
        var = head_sums(dev * dev) / HEAD_DIM
        z = dev * lax.rsqrt(var + GN_EPS) * lnw[i] + lnb[i]
        bonus = head_sums(r[i] * k[i] * rk[i]) * v[i]
        o_ref[c * L:(c + 1) * L, p * LANES:(p + 1) * LANES] = (
            (z + bonus) * tile(g_ref, c, p).astype(F32)).astype(BF16)


def _rwkv_scan(r, lw, k, v, a, g, p, batch, seq, tb=256, lanes=4 * LANES):
    n = r.shape[0]
    tb = min(tb, seq)
    nt = seq // tb
    tile = pl.BlockSpec((tb, lanes), lambda b, h, t: (b * nt + t, h))
    lane_const = pl.BlockSpec((1, lanes), lambda b, h, t: (0, h))
    return pl.pallas_call(
        _rwkv_scan_kernel,
        grid=(batch, D_MODEL // lanes, nt),
        in_specs=[tile] * 6 + [lane_const] * 5,
        out_specs=tile,
        out_shape=jax.ShapeDtypeStruct((n, D_MODEL), BF16),
        scratch_shapes=[pltpu.VMEM((HEAD_DIM, lanes), F32)],
        compiler_params=_cparams("parallel", "parallel", "arbitrary"),
    )(r, lw, k, v, a, g, p['k_k'], p['k_a'], p['r_k'], p['ln_w'], p['ln_b'])


def _pad_to(t, axis, size):
    pad = [(0, 0)] * t.ndim
    pad[axis] = (0, size - t.shape[axis])
    return jnp.pad(t, pad)


def _row(t):
    return t.reshape(1, -1).astype(F32)


def _pair_tile(gain):
    return jnp.tile(gain.astype(F32), 2).reshape(1, LANES)


def kernel(x, mix_norm, ffn_norm, ffn_w_gate, ffn_w_up, ffn_w_down, hy_w_in, hy_f_bias, hy_q_gain, hy_k_gain, hy_pool_w, hy_pool_scale, hy_w_out, rw_mu, rw_w_r, rw_w_k, rw_w_v, rw_w0, rw_w1, rw_w2, rw_a0, rw_a1, rw_a2, rw_g1, rw_g2, rw_k_k, rw_k_a, rw_r_k, rw_ln_w, rw_ln_b, rw_w_o, rw_v0, rw_v1, rw_v2):
    batch, seq, d_model = x.shape
    assert d_model == D_MODEL and seq % 256 == 0
    depth = mix_norm.shape[0]
    n = batch * seq
    xs = x.reshape(n, d_model)
    v_first = None
    for layer in range(depth):
        gain = _row(mix_norm[layer])
        if layer % 2 == 0:
            e = layer // 2
            w_in = hy_w_in[e]
            w_qg = jnp.concatenate([w_in[:, :FOX_WIDTH], w_in[:, 3 * FOX_WIDTH:4 * FOX_WIDTH]], axis=1).astype(BF16)
            w_k = w_in[:, FOX_WIDTH:2 * FOX_WIDTH].astype(BF16)
            w_u = w_in[:, 4 * FOX_WIDTH + FOX_HEADS:].astype(BF16)
            wv_t = w_in[:, 2 * FOX_WIDTH:3 * FOX_WIDTH].T.astype(BF16)
            wf = _pad_to(w_in[:, 4 * FOX_WIDTH:4 * FOX_WIDTH + FOX_HEADS], 1, LANES).astype(BF16)
            f_bias = _pad_to(_row(hy_f_bias[e]), 1, LANES)
            qg, kaug, vt, u = _inproj_even(xs, gain, w_k, w_qg, w_u, wv_t, wf, f_bias,
                                           _pair_tile(hy_k_gain[e]), seq)
            attn = _fox_attn(qg, kaug, vt, _pair_tile(hy_q_gain[e]), batch, seq)
            mixed = _pool(u, hy_pool_w[e].astype(BF16), _row(hy_pool_scale[e]), seq)
            w_out = hy_w_out[e].astype(BF16)
            mixes = [(attn, w_out[:FOX_WIDTH]), (mixed, w_out[FOX_WIDTH:])]
        else:
            o = layer // 2
            p = dict(mu=rw_mu[o].astype(F32),
                     w_r=rw_w_r[o].astype(BF16), w_k=rw_w_k[o].astype(BF16), w_v=rw_w_v[o].astype(BF16),
                     w0=_row(rw_w0[o]), w1=_pad_to(rw_w1[o], 1, LANES).astype(BF16),
                     w2=_pad_to(rw_w2[o], 0, LANES).astype(BF16),
                     a0=_row(rw_a0[o]), a1=_pad_to(rw_a1[o], 1, LANES).astype(BF16),
                     a2=_pad_to(rw_a2[o], 0, LANES).astype(BF16),
                     g1=_pad_to(rw_g1[o], 1, 2 * LANES).astype(BF16),
                     g2=_pad_to(rw_g2[o], 0, 2 * LANES).astype(BF16),
                     k_k=_row(rw_k_k[o]), k_a=_row(rw_k_a[o]), r_k=_row(rw_r_k[o]),
                     ln_w=_row(rw_ln_w[o]), ln_b=_row(rw_ln_b[o]))
            if o > 0:
                p.update(v0=_row(rw_v0[o - 1]), v1=_pad_to(rw_v1[o - 1], 1, LANES).astype(BF16),
                         v2=_pad_to(rw_v2[o - 1], 0, LANES).astype(BF16))
            r, lw, k, v, a, g = _rwkv_proj(xs, gain, p, v_first if o > 0 else None, seq)
            if o == 0:
                v_first = v
            mixes = [(_rwkv_scan(r, lw, k, v, a, g, p, batch, seq), rw_w_o[o].astype(BF16))]
        xs = _proj_ffn(xs, mixes, _row(ffn_norm[layer]), ffn_w_gate[layer].astype(BF16),
                       ffn_w_up[layer].astype(BF16), ffn_w_down[layer].astype(BF16))
    return xs.reshape(batch, seq, d_model)
```

```python
import functools

import jax
import jax.numpy as jnp
from jax import lax
from jax.experimental import pallas as pl
from jax.experimental.pallas import tpu as pltpu

F32 = jnp.float32
BF16 = jnp.bfloat16

D_MODEL = 1024
HEAD_DIM = 64
LANES = 128
FOX_HEADS = 8
FOX_WIDTH = FOX_HEADS * HEAD_DIM
FOX_PAIRS = FOX_HEADS // 2
POOL_WINDOWS = (2, 4, 8, 16)
POOL_WIDTH = D_MODEL - FOX_WIDTH
POOL_HALO = 16
RMS_EPS = 1e-6
GN_EPS = 64e-5
CHUNK = 64
DECAY_SCALE = 0.6065306597126334
NEG_BIG = -1e30
LOG2E = 1.4426950408889634
ATTN_SUB = 512
VT_ROWS = 80
VMEM_LIMIT = 56 * 1024 * 1024


def _cparams(*sem, flags=None):
    return pltpu.CompilerParams(dimension_semantics=sem, vmem_limit_bytes=VMEM_LIMIT, flags=flags)


def _rms(x, gain):
    ms = jnp.mean(x * x, axis=-1, keepdims=True)
    return x * lax.rsqrt(ms + RMS_EPS) * gain


def _dot(a, b):
    return jnp.dot(a, b, preferred_element_type=F32)


def _dot_nt(a, b):
    return lax.dot_general(a, b, (((1,), (1,)), ((), ())), preferred_element_type=F32)


def _dot_tn(a, b):
    return lax.dot_general(a, b, (((0,), (0,)), ((), ())), preferred_element_type=F32)


def _const_spec(shape):
    nd = len(shape)
    return pl.BlockSpec(shape, lambda *_: (0,) * nd, pipeline_mode=pl.Buffered(1))


def _pair_rms_scale(t, first):
    t2 = t * t
    s0 = jnp.sum(jnp.where(first, t2, 0.0), axis=-1, keepdims=True)
    s1 = jnp.sum(jnp.where(first, 0.0, t2), axis=-1, keepdims=True)
    return jnp.where(first, lax.rsqrt(s0 / HEAD_DIM + RMS_EPS), lax.rsqrt(s1 / HEAD_DIM + RMS_EPS))


def _split3_bf16(t):
    hi = t.astype(BF16)
    r = t - hi.astype(F32)
    mid = r.astype(BF16)
    return hi, mid, (r - mid.astype(F32)).astype(BF16)


def _inproj_even_kernel(x_ref, gain_ref, wk_ref, wqg_ref, wu_ref, wvt_ref, wf_ref, fb_ref, kg_ref, tri_ref,
                        qg_ref, kaug_ref, vt_ref, u_ref, carry_ref, *, tiles_per_seq):
    i = pl.program_id(0)
    hb = _rms(x_ref[...], gain_ref[...]).astype(BF16)
    z = _dot(hb, wf_ref[...]) + fb_ref[...]
    c = jnp.minimum(z, 0.0) - jnp.log1p(jnp.exp(-jnp.abs(z)))
    tm = c.shape[0]
    c3 = _dot(tri_ref[...], jnp.concatenate(_split3_bf16(c), axis=1))
    c = c3[:, :LANES] + c3[:, LANES:2 * LANES] + c3[:, 2 * LANES:]

    @pl.when(i % tiles_per_seq == 0)
    def _():
        carry_ref[...] = jnp.zeros_like(carry_ref)

    c = c + carry_ref[...]
    carry_ref[...] = c[tm - 1:tm, :]
    main = _dot(hb, wk_ref[...])
    lane = lax.broadcasted_iota(jnp.int32, (1, LANES), 1)
    first = lane < HEAD_DIM
    key_bias = c * (-LOG2E)
    for p in range(FOX_PAIRS):
        k = main[:, p * LANES:(p + 1) * LANES]
        kn = k * _pair_rms_scale(k, first) * kg_ref[...]
        for h in range(2):
            hi, mid, lo = (t.astype(F32) for t in _split3_bf16(key_bias[:, 2 * p + h:2 * p + h + 1]))
            spare = HEAD_DIM * (1 - h)
            pieces = jnp.where(lane == spare, hi, jnp.where(lane == spare + 1, mid,
                                                            jnp.where(lane == spare + 2, lo, 0.0)))
            own = first if h == 0 else jnp.logical_not(first)
            kaug_ref[:, (2 * p + h) * LANES:(2 * p + h + 1) * LANES] = jnp.where(own, kn, pieces).astype(BF16)
    qg_ref[...] = _dot(hb, wqg_ref[...]).astype(BF16)
    u_ref[...] = _dot(hb, wu_ref[...])
    v_t = _dot_nt(wvt_ref[...], hb)
    ones_row = jnp.where(lax.broadcasted_iota(jnp.int32, (VT_ROWS - HEAD_DIM, tm), 0) == 0, 1.0, 0.0).astype(BF16)
    for h in range(FOX_HEADS):
        vt_ref[h * VT_ROWS:h * VT_ROWS + HEAD_DIM, :] = v_t[h * HEAD_DIM:(h + 1) * HEAD_DIM, :].astype(BF16)
        vt_ref[h * VT_ROWS + HEAD_DIM:(h + 1) * VT_ROWS, :] = ones_row


def _inproj_even(x, gain, w_k, w_qg, w_u, wv_t, wf, f_bias, k_gain2, seq, tm=512):
    n = x.shape[0]
    tm = min(tm, seq)
    tri = jnp.tril(jnp.ones((tm, tm), BF16))
    consts = [gain, w_k, w_qg, w_u, wv_t, wf, f_bias, k_gain2, tri]
    return pl.pallas_call(
        functools.partial(_inproj_even_kernel, tiles_per_seq=seq // tm),
        grid=(n // tm,),
        in_specs=[pl.BlockSpec((tm, D_MODEL), lambda i: (i, 0))] + [_const_spec(c.shape) for c in consts],
        out_specs=[pl.BlockSpec((tm, 2 * FOX_WIDTH), lambda i: (i, 0)),
                   pl.BlockSpec((tm, 2 * FOX_WIDTH), lambda i: (i, 0)),
                   pl.BlockSpec((FOX_HEADS * VT_ROWS, tm), lambda i: (0, i)),
                   pl.BlockSpec((tm, POOL_WIDTH), lambda i: (i, 0))],
        out_shape=[jax.ShapeDtypeStruct((n, 2 * FOX_WIDTH), BF16),
                   jax.ShapeDtypeStruct((n, 2 * FOX_WIDTH), BF16),
                   jax.ShapeDtypeStruct((FOX_HEADS * VT_ROWS, n), BF16),
                   jax.ShapeDtypeStruct((n, POOL_WIDTH), F32)],
        scratch_shapes=[pltpu.VMEM((1, LANES), F32)],
        compiler_params=_cparams("arbitrary"),
    )(x, *consts)


def _fox_attn_kernel(q_ref, og_ref, k0_ref, k1_ref, vt_ref, qg_ref, o_ref, *, tq):
    qi = pl.program_id(2)
    lane = lax.broadcasted_iota(jnp.int32, (1, LANES), 1)
    first = lane < HEAD_DIM
    k_refs = (k0_ref, k1_ref)

    q = q_ref[...].astype(F32)
    qn = q * _pair_rms_scale(q, first) * (qg_ref[...] * (HEAD_DIM ** -0.5 * LOG2E))
    ones0 = jnp.where((lane >= HEAD_DIM) & (lane < HEAD_DIM + 3), 1.0, 0.0)
    ones1 = jnp.where(lane < 3, 1.0, 0.0)
    q_heads = (jnp.where(first, qn, ones0).astype(BF16), jnp.where(first, ones1, qn).astype(BF16))
    n_sub = tq // ATTN_SUB
    chains = [(u, h) for u in range(n_sub) for h in range(2)]
    q_sub = [q_heads[h][u * ATTN_SUB:(u + 1) * ATTN_SUB, :] for u, h in chains]
    key_row = lax.broadcasted_iota(jnp.int32, (tq, ATTN_SUB), 0)
    q_col = lax.broadcasted_iota(jnp.int32, (tq, ATTN_SUB), 1)

    def step(blocks, stats):
        rows = [pl.ds(pl.multiple_of(j * tq, tq), tq) for j, _ in blocks]
        s = []
        for (_, diagonal), r in zip(blocks, rows):
            sb = [_dot_nt(k_refs[h][r, :], q_sub[i]) for i, (u, h) in enumerate(chains)]
            if diagonal:
                sb = [jnp.where(key_row <= q_col + u * ATTN_SUB, x, NEG_BIG) for x, (u, h) in zip(sb, chains)]
            s.append(sb)
        m_new = [functools.reduce(jnp.maximum, [jnp.max(sb[i], axis=0, keepdims=True) for sb in s], st[0])
                 for i, st in enumerate(stats)]
        alpha = [jnp.exp2(st[0] - m) for st, m in zip(stats, m_new)]
        p = [[jnp.exp2(x - m) for x, m in zip(sb, m_new)] for sb in s]
        pv = [sum(_dot(vt_ref[h * VT_ROWS:(h + 1) * VT_ROWS, r], pb[i].astype(BF16)) for pb, r in zip(p, rows))
              for i, (u, h) in enumerate(chains)]
        return tuple((m, a * st[1] + y) for m, a, st, y in zip(m_new, alpha, stats, pv))

    def finish(stats):
        for u in range(n_sub):
            (_, a0), (_, a1) = stats[2 * u], stats[2 * u + 1]
            attn = jnp.concatenate([a0[:HEAD_DIM] / a0[HEAD_DIM:HEAD_DIM + 1],
                                    a1[:HEAD_DIM] / a1[HEAD_DIM:HEAD_DIM + 1]], axis=0).T
            rows = slice(u * ATTN_SUB, (u + 1) * ATTN_SUB)
            o_ref[rows, :] = (attn * jax.nn.sigmoid(og_ref[rows, :].astype(F32))).astype(BF16)

    init = tuple((jnp.full((1, ATTN_SUB), NEG_BIG, F32), jnp.zeros((VT_ROWS, ATTN_SUB), F32)) for _ in chains)
    stats = lax.fori_loop(0, qi // 2, lambda i, c: step([(2 * i, False), (2 * i + 1, False)], c), init)

    @pl.when(qi % 2 == 0)
    def _():
        finish(step([(qi, True)], stats))

    @pl.when(qi % 2 == 1)
    def _():
        finish(step([(qi - 1, False), (qi, True)], stats))


def _fox_attn(qg, kaug, vt, q_gain2, batch, seq, tq=512):
    n = qg.shape[0]
    tq = min(tq, seq)
    nq = seq // tq
    return pl.pallas_call(
        functools.partial(_fox_attn_kernel, tq=tq),
        grid=(batch, FOX_PAIRS, nq),
        in_specs=[pl.BlockSpec((tq, LANES), lambda b, p, i: (b * nq + i, p)),
                  pl.BlockSpec((tq, LANES), lambda b, p, i: (b * nq + i, FOX_PAIRS + p)),
                  pl.BlockSpec((seq, LANES), lambda b, p, i: (b, 2 * p)),
                  pl.BlockSpec((seq, LANES), lambda b, p, i: (b, 2 * p + 1)),
                  pl.BlockSpec((2 * VT_ROWS, seq), lambda b, p, i: (p, b)),
                  _const_spec(q_gain2.shape)],
        out_specs=pl.BlockSpec((tq, LANES), lambda b, p, i: (b * nq + i, p)),
        out_shape=jax.ShapeDtypeStruct((n, FOX_WIDTH), BF16),
        compiler_params=_cparams("parallel", "parallel", "arbitrary"),
    )(qg, qg, kaug, kaug, vt, q_gain2)


def _split_bf16(t):
    hi = t.astype(BF16)
    lo = (t - hi.astype(F32)).astype(BF16)
    return hi, lo


def _pool_kernel(u_ref, halo_ref, pw_ref, ps_ref, o_ref, *, tiles_per_seq):
    i = pl.program_id(0)
    tm = u_ref.shape[0]
    tile_in_seq = i % tiles_per_seq
    u = u_ref[...]
    halo = halo_ref[...] * (tile_in_seq != 0).astype(F32)
    row = lax.broadcasted_iota(jnp.int32, (tm, tm), 0)
    col = lax.broadcasted_iota(jnp.int32, (tm, tm), 1)
    hrow = lax.broadcasted_iota(jnp.int32, (tm, POOL_HALO), 0)
    hcol = lax.broadcasted_iota(jnp.int32, (tm, POOL_HALO), 1)
    pos = tile_in_seq * tm + lax.broadcasted_iota(jnp.int32, (tm, 1), 0)
    groups = range(len(POOL_WINDOWS))
    sl = [slice(g * LANES, (g + 1) * LANES) for g in groups]
    band = [jnp.where((col <= row) & (row - col < w), 1.0, 0.0).astype(BF16) for w in POOL_WINDOWS]
    hband = [jnp.where(hrow + POOL_HALO - hcol < w, 1.0, 0.0).astype(BF16) for w in POOL_WINDOWS]
    wsum2 = [_dot(band[g], jnp.concatenate(_split_bf16(u[:, sl[g]]), axis=1))
             + _dot(hband[g], jnp.concatenate(_split_bf16(halo[:, sl[g]]), axis=1)) for g in groups]
    pooled = [(wsum2[g][:, :LANES] + wsum2[g][:, LANES:]) / jnp.minimum(pos + 1, w).astype(F32) - u[:, sl[g]]
              for g, w in enumerate(POOL_WINDOWS)]
    mixed = [_dot(pooled[g].astype(BF16), pw_ref[g]) for g in groups]
    for g in groups:
        o_ref[:, sl[g]] = (mixed[g] * ps_ref[:, sl[g]]).astype(BF16)


def _pool(u, pool_w, pool_scale, seq, tm=512):
    n = u.shape[0]
    tm = min(tm, seq)
    hb = tm // POOL_HALO
    return pl.pallas_call(
        functools.partial(_pool_kernel, tiles_per_seq=seq // tm),
        grid=(n // tm,),
        in_specs=[pl.BlockSpec((tm, POOL_WIDTH), lambda i: (i, 0)),
                  pl.BlockSpec((POOL_HALO, POOL_WIDTH), lambda i: (jnp.maximum(i * hb - 1, 0), 0)),
                  _const_spec(pool_w.shape), _const_spec(pool_scale.shape)],
        out_specs=pl.BlockSpec((tm, POOL_WIDTH), lambda i: (i, 0)),
        out_shape=jax.ShapeDtypeStruct((n, POOL_WIDTH), BF16),
        compiler_params=_cparams("parallel"),
    )(u, u, pool_w, pool_scale)


def _proj_ffn_kernel(*refs, n_mix):
    x_ref = refs[0]
    mix_refs = refs[1:1 + 2 * n_mix]
    gain_ref, wg_ref, wu_ref, wd_ref, o_ref = refs[1 + 2 * n_mix:]
    x = x_ref[...]
    for m in range(n_mix):
        x = x + _dot(mix_refs[2 * m][...], mix_refs[2 * m + 1][...])
    hb = _rms(x, gain_ref[...]).astype(BF16)
    gate = _dot(hb, wg_ref[...])
    up = _dot(hb, wu_ref[...])
    act = (gate * jax.nn.sigmoid(gate) * up).astype(BF16)
    o_ref[...] = x + _dot(act, wd_ref[...])


def _proj_ffn(x, mixes, gain, w_gate, w_up, w_down, tm=512):
    n = x.shape[0]
    tm = min(tm, n)
    args, specs = [x], [pl.BlockSpec((tm, D_MODEL), lambda i: (i, 0))]
    for y, w in mixes:
        args += [y, w]
        specs += [pl.BlockSpec((tm, y.shape[1]), lambda i: (i, 0)), _const_spec(w.shape)]
    consts = [gain, w_gate, w_up, w_down]
    return pl.pallas_call(
        functools.partial(_proj_ffn_kernel, n_mix=len(mixes)),
        grid=(n // tm,),
        in_specs=specs + [_const_spec(c.shape) for c in consts],
        out_specs=pl.BlockSpec((tm, D_MODEL), lambda i: (i, 0)),
        out_shape=jax.ShapeDtypeStruct((n, D_MODEL), F32),
        compiler_params=_cparams("parallel"),
    )(*args, *consts)


def _rwkv_proj_kernel(*refs, tiles_per_seq, has_vmix):
    if has_vmix:
        (x_ref, halo_ref, gain_ref, mu_ref, wr_ref, wk_ref, wv_ref, w0_ref, w1_ref, w2_ref,
         a0_ref, a1_ref, a2_ref, g1_ref, g2_ref, vf_ref, v0_ref, v1_ref, v2_ref,
         r_ref, lw_ref, k_ref, v_ref, a_ref, g_ref) = refs
    else:
        (x_ref, halo_ref, gain_ref, mu_ref, wr_ref, wk_ref, wv_ref, w0_ref, w1_ref, w2_ref,
         a0_ref, a1_ref, a2_ref, g1_ref, g2_ref,
         r_ref, lw_ref, k_ref, v_ref, a_ref, g_ref) = refs
    i = pl.program_id(0)
    tm = x_ref.shape[0]
    gain = gain_ref[...]
    h = _rms(x_ref[...], gain)
    h_last = _rms(halo_ref[...], gain)[halo_ref.shape[0] - 1:, :]
    h_last = h_last * (i % tiles_per_seq != 0).astype(F32)
    rows = lax.broadcasted_iota(jnp.int32, (tm, 1), 0)
    h_prev = jnp.where(rows == 0, h_last, pltpu.roll(h, 1, axis=0))
    xx = h_prev - h

    def mix(idx):
        return (h + xx * mu_ref[idx:idx + 1, :]).astype(BF16)

    xr, xw, xk, xv, xa, xg = (mix(idx) for idx in range(6))
    r_ref[...] = _dot(xr, wr_ref[...]).astype(BF16)
    k_ref[...] = _dot(xk, wk_ref[...]).astype(BF16)
    v = _dot(xv, wv_ref[...])
    wl = w0_ref[...] + _dot(jnp.tanh(_dot(xw, w1_ref[...])).astype(BF16), w2_ref[...])
    lw_ref[...] = -DECAY_SCALE * jax.nn.sigmoid(wl)
    a_ref[...] = jax.nn.sigmoid(
        a0_ref[...] + _dot(_dot(xa, a1_ref[...]).astype(BF16), a2_ref[...])).astype(BF16)
    g_ref[...] = _dot(jax.nn.sigmoid(_dot(xg, g1_ref[...])).astype(BF16), g2_ref[...]).astype(BF16)
    if has_vmix:
        gate = jax.nn.sigmoid(v0_ref[...] + _dot(_dot(xv, v1_ref[...]).astype(BF16), v2_ref[...]))
        v = v + (vf_ref[...].astype(F32) - v) * gate
    v_ref[...] = v.astype(BF16)


def _rwkv_proj(x, gain, p, v_first, seq, tm=512):
    n = x.shape[0]
    tm = min(tm, seq)
    has_vmix = v_first is not None
    tile = pl.BlockSpec((tm, D_MODEL), lambda i: (i, 0))
    halo = pl.BlockSpec((8, D_MODEL), lambda i: (jnp.maximum(i * (tm // 8) - 1, 0), 0))
    consts = [gain, p['mu'], p['w_r'], p['w_k'], p['w_v'], p['w0'], p['w1'], p['w2'],
              p['a0'], p['a1'], p['a2'], p['g1'], p['g2']]
    args = [x, x] + consts
    specs = [tile, halo] + [_const_spec(c.shape) for c in consts]
    if has_vmix:
        extra = [p['v0'], p['v1'], p['v2']]
        args += [v_first] + extra
        specs += [tile] + [_const_spec(c.shape) for c in extra]
    out_dtypes = [BF16, F32, BF16, BF16, BF16, BF16]
    return pl.pallas_call(
        functools.partial(_rwkv_proj_kernel, tiles_per_seq=seq // tm, has_vmix=has_vmix),
        grid=(n // tm,),
        in_specs=specs,
        out_specs=[tile] * 6,
        out_shape=[jax.ShapeDtypeStruct((n, D_MODEL), dt) for dt in out_dtypes],
        compiler_params=_cparams("parallel"),
    )(*args)


def _rwkv_scan_kernel(r_ref, lw_ref, k_ref, v_ref, a_ref, g_ref, kk_ref, ka_ref, rk_ref, lnw_ref, lnb_ref,
                      o_ref, state_ref):
    t = pl.program_id(2)

    @pl.when(t == 0)
    def _():
        state_ref[...] = jnp.zeros_like(state_ref)

    L = CHUNK
    n_chunks = r_ref.shape[0] // L
    n_pairs = r_ref.shape[1] // LANES
    items = [(c, p) for c in range(n_chunks) for p in range(n_pairs)]
    first = lax.broadcasted_iota(jnp.int32, (1, LANES), 1) < HEAD_DIM
    trow = lax.broadcasted_iota(jnp.int32, (L, LANES), 0)
    tcol = lax.broadcasted_iota(jnp.int32, (L, LANES), 1) & (HEAD_DIM - 1)
    strict = tcol < trow
    incl = tcol <= trow
    same16 = (trow >> 4) == (tcol >> 4)
    same32 = (trow >> 5) == (tcol >> 5)
    eye = jnp.where(trow == tcol, 1.0, 0.0)
    same_head = ((lax.broadcasted_iota(jnp.int32, (LANES, LANES), 0) >> 6)
                 == (lax.broadcasted_iota(jnp.int32, (LANES, LANES), 1) >> 6))
    tri = jnp.where(lax.broadcasted_iota(jnp.int32, (L, L), 1) <= lax.broadcasted_iota(jnp.int32, (L, L), 0),
                    1.0, 0.0).astype(BF16)

    def bf(z):
        return z.astype(BF16)

    def stack(zb):
        return jnp.concatenate([jnp.where(first, zb, 0), jnp.where(first, 0, zb)], axis=0)

    def wmm(x, y):
        return _dot(bf(x), stack(bf(y)))

    def head_sums(z):
        s0 = jnp.sum(jnp.where(first, z, 0.0), axis=-1, keepdims=True)
        s1 = jnp.sum(jnp.where(first, 0.0, z), axis=-1, keepdims=True)
        return jnp.where(first, s0, s1)

    def tile(ref, c, p):
        return ref[c * L:(c + 1) * L, p * LANES:(p + 1) * LANES]

    def load(ref):
        return [tile(ref, c, p).astype(F32) for c, p in items]

    def lane_param(ref):
        return [ref[:, p * LANES:(p + 1) * LANES] for _, p in items]

    r, lw, kraw, v, asig = load(r_ref), load(lw_ref), load(k_ref), load(v_ref), load(a_ref)
    kk = [x * w for x, w in zip(kraw, lane_param(kk_ref))]
    kk = [x / jnp.maximum(jnp.sqrt(head_sums(x * x)), 1e-12) for x in kk]
    k = [x * (1.0 + (s - 1.0) * w) for x, s, w in zip(kraw, asig, lane_param(ka_ref))]
    b = [x * s for x, s in zip(kk, asig)]
    lw_hi = [bf(x) for x in lw]
    lw_lo = [bf(x - h.astype(F32)) for x, h in zip(lw, lw_hi)]
    cum = [_dot(tri, jnp.concatenate([h, l], axis=1)) for h, l in zip(lw_hi, lw_lo)]
    cum = [x[:, :LANES] + x[:, LANES:] for x in cum]
    cum_end = [x[L - 1:L, :] for x in cum]
    e_neg = [jnp.exp(-x) for x in cum]
    e_end = [jnp.exp(ce - x) for x, ce in zip(cum, cum_end)]
    rt = [x * jnp.exp(c) for x, c in zip(r, cum)]
    at_b = [bf(-x * jnp.exp(c - w)) for x, c, w in zip(kk, cum, lw)]
    bt = [bf(x * e) for x, e in zip(b, e_neg)]
    kt = [bf(x * e) for x, e in zip(k, e_neg)]
    bh = [bf(x * e) for x, e in zip(b, e_end)]
    kh = [bf(x * e) for x, e in zip(k, e_end)]
    v_b = [bf(x) for x in v]
    aa = [_dot_nt(jnp.concatenate([xa, bf(xr)], axis=0), jnp.concatenate([stack(xb), stack(xk)], axis=0))
          for xa, xr, xb, xk in zip(at_b, rt, bt, kt)]
    a_ab = [jnp.where(strict, x[:L, :LANES], 0.0) for x in aa]
    a_ak = [jnp.where(strict, x[:L, LANES:], 0.0) for x in aa]
    a_rb = [jnp.where(incl, x[L:, :LANES], 0.0) for x in aa]
    a_rk = [jnp.where(incl, x[L:, LANES:], 0.0) for x in aa]
    pw = [jnp.where(same16, x, 0.0) for x in a_ab]
    tinv = [eye + x for x in pw]
    pw = [wmm(x, x) for x in pw]
    for _ in range(2):
        both = [_dot(jnp.concatenate([bf(x), bf(p)], axis=0), stack(bf(p))) for x, p in zip(tinv, pw)]
        tinv = [x + y[:L] for x, y in zip(tinv, both)]
        pw = [y[L:] for y in both]
    tinv = [x + wmm(x, p) for x, p in zip(tinv, pw)]
    off = [jnp.where(same32 & ~same16, x, 0.0) for x in a_ab]
    tinv = [x + wmm(wmm(x, o), x) for x, o in zip(tinv, off)]
    off = [jnp.where(same32, 0.0, x) for x in a_ab]
    tinv = [x + wmm(wmm(x, o), x) for x, o in zip(tinv, off)]
    akv = [wmm(x, y) for x, y in zip(a_ak, v)]
    x12 = [_dot(bf(x), jnp.concatenate([stack(y), stack(bf(z))], axis=1))
           for x, y, z in zip(tinv, at_b, akv)]
    x1 = [bf(x[:, :LANES]) for x in x12]
    x2 = [bf(x[:, LANES:]) for x in x12]
    ab12 = [_dot(bf(x), jnp.concatenate([stack(y), stack(z)], axis=1)) for x, y, z in zip(a_rb, x1, x2)]
    y1 = [bf(x + y[:, :LANES]) for x, y in zip(rt, ab12)]
    y2 = [y[:, LANES:] + wmm(x, z) for y, x, z in zip(ab12, a_rk, v)]
    m_low = [bf(jnp.where(same_head, _dot_tn(x, y), 0.0)) for x, y in zip(x1, bh)]
    c_full = [_dot_tn(jnp.concatenate([x, y], axis=0), jnp.concatenate([z, w], axis=0))
              for x, y, z, w in zip(x2, v_b, bh, kh)]
    c_wide = [jnp.where(first, x[:L], x[L:]) for x in c_full]
    p_end = [jnp.exp(x) for x in cum_end]

    states = [state_ref[:, p * LANES:(p + 1) * LANES] for p in range(n_pairs)]
    y = []
    for i, (c, p) in enumerate(items):
        state_b = bf(states[p])
        y.append(_dot_nt(y1[i], stack(state_b)) + y2[i])
        states[p] = states[p] * p_end[i] + _dot(state_b, m_low[i]) + c_wide[i]
    for p in range(n_pairs):
        state_ref[:, p * LANES:(p + 1) * LANES] = states[p]

    lnw, lnb, rk = lane_param(lnw_ref), lane_param(lnb_ref), lane_param(rk_ref)
    for i, (c, p) in enumerate(items):
        mean = head_sums(y[i]) / HEAD_DIM
        dev = y[i] - mean
        var = head_sums(dev * dev) / HEAD_DIM
        z = dev * lax.rsqrt(var + GN_EPS) * lnw[i] + lnb[i]
        bonus = head_sums(r[i] * k[i] * rk[i]) * v[i]
        o_ref[c * L:(c + 1) * L, p * LANES:(p + 1) * LANES] = (
            (z + bonus) * tile(g_ref, c, p).astype(F32)).astype(BF16)


def _rwkv_scan(r, lw, k, v, a, g, p, batch, seq, tb=256, lanes=4 * LANES):
    n = r.shape[0]
    tb = min(tb, seq)
    nt = seq // tb
    tile = pl.BlockSpec((tb, lanes), lambda b, h, t: (b * nt + t, h))
    lane_const = pl.BlockSpec((1, lanes), lambda b, h, t: (0, h))
    return pl.pallas_call(
        _rwkv_scan_kernel,
        grid=(batch, D_MODEL // lanes, nt),
        in_specs=[tile] * 6 + [lane_const] * 5,
        out_specs=tile,
        out_shape=jax.ShapeDtypeStruct((n, D_MODEL), BF16),
        scratch_shapes=[pltpu.VMEM((HEAD_DIM, lanes), F32)],
        compiler_params=_cparams("parallel", "parallel", "arbitrary"),
    )(r, lw, k, v, a, g, p['k_k'], p['k_a'], p['r_k'], p['ln_w'], p['ln_b'])


def _pad_to(t, axis, size):
    pad = [(0, 0)] * t.ndim
    pad[axis] = (0, size - t.shape[axis])
    return jnp.pad(t, pad)


def _row(t):
    return t.reshape(1, -1).astype(F32)


def _pair_tile(gain):
    return jnp.tile(gain.astype(F32), 2).reshape(1, LANES)


def kernel(x, mix_norm, ffn_norm, ffn_w_gate, ffn_w_up, ffn_w_down, hy_w_in, hy_f_bias, hy_q_gain, hy_k_gain, hy_pool_w, hy_pool_scale, hy_w_out, rw_mu, rw_w_r, rw_w_k, rw_w_v, rw_w0, rw_w1, rw_w2, rw_a0, rw_a1, rw_a2, rw_g1, rw_g2, rw_k_k, rw_k_a, rw_r_k, rw_ln_w, rw_ln_b, rw_w_o, rw_v0, rw_v1, rw_v2):
    batch, seq, d_model = x.shape
    assert d_model == D_MODEL and seq % 256 == 0
    depth = mix_norm.shape[0]
    n = batch * seq
    xs = x.reshape(n, d_model)
    v_first = None
    for layer in range(depth):
        gain = _row(mix_norm[layer])
        if layer % 2 == 0:
            e = layer // 2
            w_in = hy_w_in[e]
            w_qg = jnp.concatenate([w_in[:, :FOX_WIDTH], w_in[:, 3 * FOX_WIDTH:4 * FOX_WIDTH]], axis=1).astype(BF16)
            w_k = w_in[:, FOX_WIDTH:2 * FOX_WIDTH].astype(BF16)
            w_u = w_in[:, 4 * FOX_WIDTH + FOX_HEADS:].astype(BF16)
            wv_t = w_in[:, 2 * FOX_WIDTH:3 * FOX_WIDTH].T.astype(BF16)
            wf = _pad_to(w_in[:, 4 * FOX_WIDTH:4 * FOX_WIDTH + FOX_HEADS], 1, LANES).astype(BF16)
            f_bias = _pad_to(_row(hy_f_bias[e]), 1, LANES)
            qg, kaug, vt, u = _inproj_even(xs, gain, w_k, w_qg, w_u, wv_t, wf, f_bias,
                                           _pair_tile(hy_k_gain[e]), seq)
            attn = _fox_attn(qg, kaug, vt, _pair_tile(hy_q_gain[e]), batch, seq)
            mixed = _pool(u, hy_pool_w[e].astype(BF16), _row(hy_pool_scale[e]), seq)
            w_out = hy_w_out[e].astype(BF16)
            mixes = [(attn, w_out[:FOX_WIDTH]), (mixed, w_out[FOX_WIDTH:])]
        else:
            o = layer // 2
            p = dict(mu=rw_mu[o].astype(F32),
                     w_r=rw_w_r[o].astype(BF16), w_k=rw_w_k[o].astype(BF16), w_v=rw_w_v[o].astype(BF16),
                     w0=_row(rw_w0[o]), w1=_pad_to(rw_w1[o], 1, LANES).astype(BF16),
                     w2=_pad_to(rw_w2[o], 0, LANES).astype(BF16),
                     a0=_row(rw_a0[o]), a1=_pad_to(rw_a1[o], 1, LANES).astype(BF16),
                     a2=_pad_to(rw_a2[o], 0, LANES).astype(BF16),
                     g1=_pad_to(rw_g1[o], 1, 2 * LANES).astype(BF16),
                     g2=_pad_to(rw_g2[o], 0, 2 * LANES).astype(BF16),
                     k_k=_row(rw_k_k[o]), k_a=_row(rw_k_a[o]), r_k=_row(rw_r_k[o]),
                     ln_w=_row(rw_ln_w[o]), ln_b=_row(rw_ln_b[o]))
            if o > 0:
                p.update(v0=_row(rw_v0[o - 1]), v1=_pad_to(rw_v1[o - 1], 1, LANES).astype(BF16),
                         v2=_pad_to(rw_v2[o - 1], 0, LANES).astype(BF16))
            r, lw, k, v, a, g = _rwkv_proj(xs, gain, p, v_first if o > 0 else None, seq)
            if o == 0:
                v_first = v
            mixes = [(_rwkv_scan(r, lw, k, v, a, g, p, batch, seq), rw_w_o[o].astype(BF16))]
        xs = _proj_ffn(xs, mixes, _row(ffn_norm[layer]), ffn_w_gate[layer].astype(BF16),
                       ffn_w_up[layer].astype(BF16), ffn_w_down[layer].astype(BF16))
    return xs.reshape(batch, seq, d_model)
```

```python
import functools

import jax
import jax.numpy as jnp
from jax import lax
from jax.experimental import pallas as pl
from jax.experimental.pallas import tpu as pltpu

F32 = jnp.float32
BF16 = jnp.bfloat16

D_MODEL = 1024
HEAD_DIM = 64
LANES = 128
FOX_HEADS = 8
FOX_WIDTH = FOX_HEADS * HEAD_DIM
FOX_PAIRS = FOX_HEADS // 2
POOL_WINDOWS = (2, 4, 8, 16)
POOL_WIDTH = D_MODEL - FOX_WIDTH
POOL_HALO = 16
RMS_EPS = 1e-6
GN_EPS = 64e-5
CHUNK = 64
DECAY_SCALE = 0.6065306597126334
NEG_BIG = -1e30
LOG2E = 1.4426950408889634
ATTN_SUB = 512
ATTN_GROUP = 4
VT_ROWS = 80
VMEM_LIMIT = 56 * 1024 * 1024


def _cparams(*sem, flags=None):
    return pltpu.CompilerParams(dimension_semantics=sem, vmem_limit_bytes=VMEM_LIMIT, flags=flags)


def _rms(x, gain):
    ms = jnp.mean(x * x, axis=-1, keepdims=True)
    return x * lax.rsqrt(ms + RMS_EPS) * gain


def _dot(a, b):
    return jnp.dot(a, b, preferred_element_type=F32)


def _dot_nt(a, b):
    return lax.dot_general(a, b, (((1,), (1,)), ((), ())), preferred_element_type=F32)


def _dot_tn(a, b):
    return lax.dot_general(a, b, (((0,), (0,)), ((), ())), preferred_element_type=F32)


def _const_spec(shape):
    nd = len(shape)
    return pl.BlockSpec(shape, lambda *_: (0,) * nd, pipeline_mode=pl.Buffered(1))


def _pair_rms_scale(t, first):
    t2 = t * t
    s0 = jnp.sum(jnp.where(first, t2, 0.0), axis=-1, keepdims=True)
    s1 = jnp.sum(jnp.where(first, 0.0, t2), axis=-1, keepdims=True)
    return jnp.where(first, lax.rsqrt(s0 / HEAD_DIM + RMS_EPS), lax.rsqrt(s1 / HEAD_DIM + RMS_EPS))


def _split3_bf16(t):
    hi = t.astype(BF16)
    r = t - hi.astype(F32)
    mid = r.astype(BF16)
    return hi, mid, (r - mid.astype(F32)).astype(BF16)


def _inproj_even_kernel(x_ref, gain_ref, wk_ref, wqg_ref, wu_ref, wvt_ref, wf_ref, fb_ref, kg_ref, tri_ref,
                        qg_ref, kaug_ref, vt_ref, u_ref, carry_ref, *, tiles_per_seq):
    i = pl.program_id(0)
    hb = _rms(x_ref[...], gain_ref[...]).astype(BF16)
    z = _dot(hb, wf_ref[...]) + fb_ref[...]
    c = jnp.minimum(z, 0.0) - jnp.log1p(jnp.exp(-jnp.abs(z)))
    tm = c.shape[0]
    c3 = _dot(tri_ref[...], jnp.concatenate(_split3_bf16(c), axis=1))
    c = c3[:, :LANES] + c3[:, LANES:2 * LANES] + c3[:, 2 * LANES:]

    @pl.when(i % tiles_per_seq == 0)
    def _():
        carry_ref[...] = jnp.zeros_like(carry_ref)

    c = c + carry_ref[...]
    carry_ref[...] = c[tm - 1:tm, :]
    main = _dot(hb, wk_ref[...])
    lane = lax.broadcasted_iota(jnp.int32, (1, LANES), 1)
    first = lane < HEAD_DIM
    key_bias = c * (-LOG2E)
    for p in range(FOX_PAIRS):
        k = main[:, p * LANES:(p + 1) * LANES]
        kn = k * _pair_rms_scale(k, first) * kg_ref[...]
        for h in range(2):
            hi, mid, lo = (t.astype(F32) for t in _split3_bf16(key_bias[:, 2 * p + h:2 * p + h + 1]))
            spare = HEAD_DIM * (1 - h)
            pieces = jnp.where(lane == spare, hi, jnp.where(lane == spare + 1, mid,
                                                            jnp.where(lane == spare + 2, lo, 0.0)))
            own = first if h == 0 else jnp.logical_not(first)
            kaug_ref[:, (2 * p + h) * LANES:(2 * p + h + 1) * LANES] = jnp.where(own, kn, pieces).astype(BF16)
    qg_ref[...] = _dot(hb, wqg_ref[...]).astype(BF16)
    u_ref[...] = _dot(hb, wu_ref[...])
    v_t = _dot_nt(wvt_ref[...], hb)
    ones_row = jnp.where(lax.broadcasted_iota(jnp.int32, (VT_ROWS - HEAD_DIM, tm), 0) == 0, 1.0, 0.0).astype(BF16)
    for h in range(FOX_HEADS):
        vt_ref[h * VT_ROWS:h * VT_ROWS + HEAD_DIM, :] = v_t[h * HEAD_DIM:(h + 1) * HEAD_DIM, :].astype(BF16)
        vt_ref[h * VT_ROWS + HEAD_DIM:(h + 1) * VT_ROWS, :] = ones_row


def _inproj_even(x, gain, w_k, w_qg, w_u, wv_t, wf, f_bias, k_gain2, seq, tm=512):
    n = x.shape[0]
    tm = min(tm, seq)
    tri = jnp.tril(jnp.ones((tm, tm), BF16))
    consts = [gain, w_k, w_qg, w_u, wv_t, wf, f_bias, k_gain2, tri]
    return pl.pallas_call(
        functools.partial(_inproj_even_kernel, tiles_per_seq=seq // tm),
        grid=(n // tm,),
        in_specs=[pl.BlockSpec((tm, D_MODEL), lambda i: (i, 0))] + [_const_spec(c.shape) for c in consts],
        out_specs=[pl.BlockSpec((tm, 2 * FOX_WIDTH), lambda i: (i, 0)),
                   pl.BlockSpec((tm, 2 * FOX_WIDTH), lambda i: (i, 0)),
                   pl.BlockSpec((FOX_HEADS * VT_ROWS, tm), lambda i: (0, i)),
                   pl.BlockSpec((tm, POOL_WIDTH), lambda i: (i, 0))],
        out_shape=[jax.ShapeDtypeStruct((n, 2 * FOX_WIDTH), BF16),
                   jax.ShapeDtypeStruct((n, 2 * FOX_WIDTH), BF16),
                   jax.ShapeDtypeStruct((FOX_HEADS * VT_ROWS, n), BF16),
                   jax.ShapeDtypeStruct((n, POOL_WIDTH), F32)],
        scratch_shapes=[pltpu.VMEM((1, LANES), F32)],
        compiler_params=_cparams("arbitrary"),
    )(x, *consts)


def _fox_attn_kernel(q_ref, og_ref, k0_ref, k1_ref, vt_ref, qg_ref, o_ref, *, tq):
    qi = pl.program_id(2)
    lane = lax.broadcasted_iota(jnp.int32, (1, LANES), 1)
    first = lane < HEAD_DIM
    k_refs = (k0_ref, k1_ref)

    q = q_ref[...].astype(F32)
    qn = q * _pair_rms_scale(q, first) * (qg_ref[...] * (HEAD_DIM ** -0.5 * LOG2E))
    ones0 = jnp.where((lane >= HEAD_DIM) & (lane < HEAD_DIM + 3), 1.0, 0.0)
    ones1 = jnp.where(lane < 3, 1.0, 0.0)
    q_heads = (jnp.where(first, qn, ones0).astype(BF16), jnp.where(first, ones1, qn).astype(BF16))
    n_sub = tq // ATTN_SUB
    chains = [(u, h) for u in range(n_sub) for h in range(2)]
    q_sub = [q_heads[h][u * ATTN_SUB:(u + 1) * ATTN_SUB, :] for u, h in chains]
    key_row = lax.broadcasted_iota(jnp.int32, (tq, ATTN_SUB), 0)
    q_col = lax.broadcasted_iota(jnp.int32, (tq, ATTN_SUB), 1)

    def step(blocks, stats):
        rows = [pl.ds(pl.multiple_of(j * tq, tq), tq) for j, _ in blocks]
        s = []
        for (_, diagonal), r in zip(blocks, rows):
            sb = [_dot_nt(k_refs[h][r, :], q_sub[i]) for i, (u, h) in enumerate(chains)]
            if diagonal:
                sb = [jnp.where(key_row <= q_col + u * ATTN_SUB, x, NEG_BIG) for x, (u, h) in zip(sb, chains)]
            s.append(sb)
        m_new = [functools.reduce(jnp.maximum, [jnp.max(sb[i], axis=0, keepdims=True) for sb in s], st[0])
                 for i, st in enumerate(stats)]
        alpha = [jnp.exp2(st[0] - m) for st, m in zip(stats, m_new)]
        p = [[jnp.exp2(x - m) for x, m in zip(sb, m_new)] for sb in s]
        pv = [sum(_dot(vt_ref[h * VT_ROWS:(h + 1) * VT_ROWS, r], pb[i].astype(BF16)) for pb, r in zip(p, rows))
              for i, (u, h) in enumerate(chains)]
        return tuple((m, a * st[1] + y) for m, a, st, y in zip(m_new, alpha, stats, pv))

    def finish(stats):
        for u in range(n_sub):
            (_, a0), (_, a1) = stats[2 * u], stats[2 * u + 1]
            attn = jnp.concatenate([a0[:HEAD_DIM] / a0[HEAD_DIM:HEAD_DIM + 1],
                                    a1[:HEAD_DIM] / a1[HEAD_DIM:HEAD_DIM + 1]], axis=0).T
            rows = slice(u * ATTN_SUB, (u + 1) * ATTN_SUB)
            o_ref[rows, :] = (attn * jax.nn.sigmoid(og_ref[rows, :].astype(F32))).astype(BF16)

    init = tuple((jnp.full((1, ATTN_SUB), NEG_BIG, F32), jnp.zeros((VT_ROWS, ATTN_SUB), F32)) for _ in chains)
    stats = lax.fori_loop(0, qi // ATTN_GROUP,
                          lambda i, c: step([(ATTN_GROUP * i + g, False) for g in range(ATTN_GROUP)], c), init)
    for left in range(ATTN_GROUP):
        @pl.when(qi % ATTN_GROUP == left)
        def _(left=left):
            finish(step([(qi - left + g, False) for g in range(left)] + [(qi, True)], stats))


def _fox_attn(qg, kaug, vt, q_gain2, batch, seq, tq=512):
    n = qg.shape[0]
    tq = min(tq, seq)
    nq = seq // tq
    return pl.pallas_call(
        functools.partial(_fox_attn_kernel, tq=tq),
        grid=(batch, FOX_PAIRS, nq),
        in_specs=[pl.BlockSpec((tq, LANES), lambda b, p, i: (b * nq + i, p)),
                  pl.BlockSpec((tq, LANES), lambda b, p, i: (b * nq + i, FOX_PAIRS + p)),
                  pl.BlockSpec((seq, LANES), lambda b, p, i: (b, 2 * p)),
                  pl.BlockSpec((seq, LANES), lambda b, p, i: (b, 2 * p + 1)),
                  pl.BlockSpec((2 * VT_ROWS, seq), lambda b, p, i: (p, b)),
                  _const_spec(q_gain2.shape)],
        out_specs=pl.BlockSpec((tq, LANES), lambda b, p, i: (b * nq + i, p)),
        out_shape=jax.ShapeDtypeStruct((n, FOX_WIDTH), BF16),
        compiler_params=_cparams("parallel", "parallel", "arbitrary"),
    )(qg, qg, kaug, kaug, vt, q_gain2)


def _split_bf16(t):
    hi = t.astype(BF16)
    lo = (t - hi.astype(F32)).astype(BF16)
    return hi, lo


def _pool_kernel(u_ref, halo_ref, pw_ref, ps_ref, o_ref, *, tiles_per_seq):
    i = pl.program_id(0)
    tm = u_ref.shape[0]
    tile_in_seq = i % tiles_per_seq
    u = u_ref[...]
    halo = halo_ref[...] * (tile_in_seq != 0).astype(F32)
    row = lax.broadcasted_iota(jnp.int32, (tm, tm), 0)
    col = lax.broadcasted_iota(jnp.int32, (tm, tm), 1)
    hrow = lax.broadcasted_iota(jnp.int32, (tm, POOL_HALO), 0)
    hcol = lax.broadcasted_iota(jnp.int32, (tm, POOL_HALO), 1)
    pos = tile_in_seq * tm + lax.broadcasted_iota(jnp.int32, (tm, 1), 0)
    groups = range(len(POOL_WINDOWS))
    sl = [slice(g * LANES, (g + 1) * LANES) for g in groups]
    band = [jnp.where((col <= row) & (row - col < w), 1.0, 0.0).astype(BF16) for w in POOL_WINDOWS]
    hband = [jnp.where(hrow + POOL_HALO - hcol < w, 1.0, 0.0).astype(BF16) for w in POOL_WINDOWS]
    wsum2 = [_dot(band[g], jnp.concatenate(_split_bf16(u[:, sl[g]]), axis=1))
             + _dot(hband[g], jnp.concatenate(_split_bf16(halo[:, sl[g]]), axis=1)) for g in groups]
    pooled = [(wsum2[g][:, :LANES] + wsum2[g][:, LANES:]) / jnp.minimum(pos + 1, w).astype(F32) - u[:, sl[g]]
              for g, w in enumerate(POOL_WINDOWS)]
    mixed = [_dot(pooled[g].astype(BF16), pw_ref[g]) for g in groups]
    for g in groups:
        o_ref[:, sl[g]] = (mixed[g] * ps_ref[:, sl[g]]).astype(BF16)


def _pool(u, pool_w, pool_scale, seq, tm=512):
    n = u.shape[0]
    tm = min(tm, seq)
    hb = tm // POOL_HALO
    return pl.pallas_call(
        functools.partial(_pool_kernel, tiles_per_seq=seq // tm),
        grid=(n // tm,),
        in_specs=[pl.BlockSpec((tm, POOL_WIDTH), lambda i: (i, 0)),
                  pl.BlockSpec((POOL_HALO, POOL_WIDTH), lambda i: (jnp.maximum(i * hb - 1, 0), 0)),
                  _const_spec(pool_w.shape), _const_spec(pool_scale.shape)],
        out_specs=pl.BlockSpec((tm, POOL_WIDTH), lambda i: (i, 0)),
        out_shape=jax.ShapeDtypeStruct((n, POOL_WIDTH), BF16),
        compiler_params=_cparams("parallel"),
    )(u, u, pool_w, pool_scale)


def _proj_ffn_kernel(*refs, n_mix):
    x_ref = refs[0]
    mix_refs = refs[1:1 + 2 * n_mix]
    gain_ref, wg_ref, wu_ref, wd_ref, o_ref = refs[1 + 2 * n_mix:]
    x = x_ref[...]
    for m in range(n_mix):
        x = x + _dot(mix_refs[2 * m][...], mix_refs[2 * m + 1][...])
    hb = _rms(x, gain_ref[...]).astype(BF16)
    gate = _dot(hb, wg_ref[...])
    up = _dot(hb, wu_ref[...])
    act = (gate * jax.nn.sigmoid(gate) * up).astype(BF16)
    o_ref[...] = x + _dot(act, wd_ref[...])


def _proj_ffn(x, mixes, gain, w_gate, w_up, w_down, tm=512):
    n = x.shape[0]
    tm = min(tm, n)
    args, specs = [x], [pl.BlockSpec((tm, D_MODEL), lambda i: (i, 0))]
    for y, w in mixes:
        args += [y, w]
        specs += [pl.BlockSpec((tm, y.shape[1]), lambda i: (i, 0)), _const_spec(w.shape)]
    consts = [gain, w_gate, w_up, w_down]
    return pl.pallas_call(
        functools.partial(_proj_ffn_kernel, n_mix=len(mixes)),
        grid=(n // tm,),
        in_specs=specs + [_const_spec(c.shape) for c in consts],
        out_specs=pl.BlockSpec((tm, D_MODEL), lambda i: (i, 0)),
        out_shape=jax.ShapeDtypeStruct((n, D_MODEL), F32),
        compiler_params=_cparams("parallel"),
    )(*args, *consts)


def _rwkv_proj_kernel(*refs, tiles_per_seq, has_vmix):
    if has_vmix:
        (x_ref, halo_ref, gain_ref, mu_ref, wr_ref, wk_ref, wv_ref, w0_ref, w1_ref, w2_ref,
         a0_ref, a1_ref, a2_ref, g1_ref, g2_ref, vf_ref, v0_ref, v1_ref, v2_ref,
         r_ref, lw_ref, k_ref, v_ref, a_ref, g_ref) = refs
    else:
        (x_ref, halo_ref, gain_ref, mu_ref, wr_ref, wk_ref, wv_ref, w0_ref, w1_ref, w2_ref,
         a0_ref, a1_ref, a2_ref, g1_ref, g2_ref,
         r_ref, lw_ref, k_ref, v_ref, a_ref, g_ref) = refs
    i = pl.program_id(0)
    tm = x_ref.shape[0]
    gain = gain_ref[...]
    h = _rms(x_ref[...], gain)
    h_last = _rms(halo_ref[...], gain)[halo_ref.shape[0] - 1:, :]
    h_last = h_last * (i % tiles_per_seq != 0).astype(F32)
    rows = lax.broadcasted_iota(jnp.int32, (tm, 1), 0)
    h_prev = jnp.where(rows == 0, h_last, pltpu.roll(h, 1, axis=0))
    xx = h_prev - h

    def mix(idx):
        return (h + xx * mu_ref[idx:idx + 1, :]).astype(BF16)

    xr, xw, xk, xv, xa, xg = (mix(idx) for idx in range(6))
    r_ref[...] = _dot(xr, wr_ref[...]).astype(BF16)
    k_ref[...] = _dot(xk, wk_ref[...]).astype(BF16)
    v = _dot(xv, wv_ref[...])
    wl = w0_ref[...] + _dot(jnp.tanh(_dot(xw, w1_ref[...])).astype(BF16), w2_ref[...])
    lw_ref[...] = -DECAY_SCALE * jax.nn.sigmoid(wl)
    a_ref[...] = jax.nn.sigmoid(
        a0_ref[...] + _dot(_dot(xa, a1_ref[...]).astype(BF16), a2_ref[...])).astype(BF16)
    g_ref[...] = _dot(jax.nn.sigmoid(_dot(xg, g1_ref[...])).astype(BF16), g2_ref[...]).astype(BF16)
    if has_vmix:
        gate = jax.nn.sigmoid(v0_ref[...] + _dot(_dot(xv, v1_ref[...]).astype(BF16), v2_ref[...]))
        v = v + (vf_ref[...].astype(F32) - v) * gate
    v_ref[...] = v.astype(BF16)


def _rwkv_proj(x, gain, p, v_first, seq, tm=512):
    n = x.shape[0]
    tm = min(tm, seq)
    has_vmix = v_first is not None
    tile = pl.BlockSpec((tm, D_MODEL), lambda i: (i, 0))
    halo = pl.BlockSpec((8, D_MODEL), lambda i: (jnp.maximum(i * (tm // 8) - 1, 0), 0))
    consts = [gain, p['mu'], p['w_r'], p['w_k'], p['w_v'], p['w0'], p['w1'], p['w2'],
              p['a0'], p['a1'], p['a2'], p['g1'], p['g2']]
    args = [x, x] + consts
    specs = [tile, halo] + [_const_spec(c.shape) for c in consts]
    if has_vmix:
        extra = [p['v0'], p['v1'], p['v2']]
        args += [v_first] + extra
        specs += [tile] + [_const_spec(c.shape) for c in extra]
    out_dtypes = [BF16, F32, BF16, BF16, BF16, BF16]
    return pl.pallas_call(
        functools.partial(_rwkv_proj_kernel, tiles_per_seq=seq // tm, has_vmix=has_vmix),
        grid=(n // tm,),
        in_specs=specs,
        out_specs=[tile] * 6,
        out_shape=[jax.ShapeDtypeStruct((n, D_MODEL), dt) for dt in out_dtypes],
        compiler_params=_cparams("parallel"),
    )(*args)


def _rwkv_scan_kernel(r_ref, lw_ref, k_ref, v_ref, a_ref, g_ref, kk_ref, ka_ref, rk_ref, lnw_ref, lnb_ref,
                      o_ref, state_ref):
    t = pl.program_id(2)

    @pl.when(t == 0)
    def _():
        state_ref[...] = jnp.zeros_like(state_ref)

    L = CHUNK
    n_chunks = r_ref.shape[0] // L
    n_pairs = r_ref.shape[1] // LANES
    items = [(c, p) for c in range(n_chunks) for p in range(n_pairs)]
    first = lax.broadcasted_iota(jnp.int32, (1, LANES), 1) < HEAD_DIM
    trow = lax.broadcasted_iota(jnp.int32, (L, LANES), 0)
    tcol = lax.broadcasted_iota(jnp.int32, (L, LANES), 1) & (HEAD_DIM - 1)
    strict = tcol < trow
    incl = tcol <= trow
    same16 = (trow >> 4) == (tcol >> 4)
    same32 = (trow >> 5) == (tcol >> 5)
    eye = jnp.where(trow == tcol, 1.0, 0.0)
    same_head = ((lax.broadcasted_iota(jnp.int32, (LANES, LANES), 0) >> 6)
                 == (lax.broadcasted_iota(jnp.int32, (LANES, LANES), 1) >> 6))
    tri = jnp.where(lax.broadcasted_iota(jnp.int32, (L, L), 1) <= lax.broadcasted_iota(jnp.int32, (L, L), 0),
                    1.0, 0.0).astype(BF16)

    def bf(z):
        return z.astype(BF16)

    def stack(zb):
        return jnp.concatenate([jnp.where(first, zb, 0), jnp.where(first, 0, zb)], axis=0)

    def wmm(x, y):
        return _dot(bf(x), stack(bf(y)))

    def head_sums(z):
        s0 = jnp.sum(jnp.where(first, z, 0.0), axis=-1, keepdims=True)
        s1 = jnp.sum(jnp.where(first, 0.0, z), axis=-1, keepdims=True)
        return jnp.where(first, s0, s1)

    def tile(ref, c, p):
        return ref[c * L:(c + 1) * L, p * LANES:(p + 1) * LANES]

    def load(ref):
        return [tile(ref, c, p).astype(F32) for c, p in items]

    def lane_param(ref):
        return [ref[:, p * LANES:(p + 1) * LANES] for _, p in items]

    r, lw, kraw, v, asig = load(r_ref), load(lw_ref), load(k_ref), load(v_ref), load(a_ref)
    kk = [x * w for x, w in zip(kraw, lane_param(kk_ref))]
    kk = [x / jnp.maximum(jnp.sqrt(head_sums(x * x)), 1e-12) for x in kk]
    k = [x * (1.0 + (s - 1.0) * w) for x, s, w in zip(kraw, asig, lane_param(ka_ref))]
    b = [x * s for x, s in zip(kk, asig)]
    lw_hi = [bf(x) for x in lw]
    lw_lo = [bf(x - h.astype(F32)) for x, h in zip(lw, lw_hi)]
    cum = [_dot(tri, jnp.concatenate([h, l], axis=1)) for h, l in zip(lw_hi, lw_lo)]
    cum = [x[:, :LANES] + x[:, LANES:] for x in cum]
    cum_end = [x[L - 1:L, :] for x in cum]
    e_neg = [jnp.exp(-x) for x in cum]
    e_end = [jnp.exp(ce - x) for x, ce in zip(cum, cum_end)]
    rt = [x * jnp.exp(c) for x, c in zip(r, cum)]
    at_b = [bf(-x * jnp.exp(c - w)) for x, c, w in zip(kk, cum, lw)]
    bt = [bf(x * e) for x, e in zip(b, e_neg)]
    kt = [bf(x * e) for x, e in zip(k, e_neg)]
    bh = [bf(x * e) for x, e in zip(b, e_end)]
    kh = [bf(x * e) for x, e in zip(k, e_end)]
    v_b = [bf(x) for x in v]
    aa = [_dot_nt(jnp.concatenate([xa, bf(xr)], axis=0), jnp.concatenate([stack(xb), stack(xk)], axis=0))
          for xa, xr, xb, xk in zip(at_b, rt, bt, kt)]
    a_ab = [jnp.where(strict, x[:L, :LANES], 0.0) for x in aa]
    a_ak = [jnp.where(strict, x[:L, LANES:], 0.0) for x in aa]
    a_rb = [jnp.where(incl, x[L:, :LANES], 0.0) for x in aa]
    a_rk = [jnp.where(incl, x[L:, LANES:], 0.0) for x in aa]
    pw = [jnp.where(same16, x, 0.0) for x in a_ab]
    tinv = [eye + x for x in pw]
    pw = [wmm(x, x) for x in pw]
    for _ in range(2):
        both = [_dot(jnp.concatenate([bf(x), bf(p)], axis=0), stack(bf(p))) for x, p in zip(tinv, pw)]
        tinv = [x + y[:L] for x, y in zip(tinv, both)]
        pw = [y[L:] for y in both]
    tinv = [x + wmm(x, p) for x, p in zip(tinv, pw)]
    off = [jnp.where(same32 & ~same16, x, 0.0) for x in a_ab]
    tinv = [x + wmm(wmm(x, o), x) for x, o in zip(tinv, off)]
    off = [jnp.where(same32, 0.0, x) for x in a_ab]
    tinv = [x + wmm(wmm(x, o), x) for x, o in zip(tinv, off)]
    akv = [wmm(x, y) for x, y in zip(a_ak, v)]
    x12 = [_dot(bf(x), jnp.concatenate([stack(y), stack(bf(z))], axis=1))
           for x, y, z in zip(tinv, at_b, akv)]
    x1 = [bf(x[:, :LANES]) for x in x12]
    x2 = [bf(x[:, LANES:]) for x in x12]
    ab12 = [_dot(bf(x), jnp.concatenate([stack(y), stack(z)], axis=1)) for x, y, z in zip(a_rb, x1, x2)]
    y1 = [bf(x + y[:, :LANES]) for x, y in zip(rt, ab12)]
    y2 = [y[:, LANES:] + wmm(x, z) for y, x, z in zip(ab12, a_rk, v)]
    m_low = [bf(jnp.where(same_head, _dot_tn(x, y), 0.0)) for x, y in zip(x1, bh)]
    c_full = [_dot_tn(jnp.concatenate([x, y], axis=0), jnp.concatenate([z, w], axis=0))
              for x, y, z, w in zip(x2, v_b, bh, kh)]
    c_wide = [jnp.where(first, x[:L], x[L:]) for x in c_full]
    p_end = [jnp.exp(x) for x in cum_end]

    states = [state_ref[:, p * LANES:(p + 1) * LANES] for p in range(n_pairs)]
    y = []
    for i, (c, p) in enumerate(items):
        state_b = bf(states[p])
        y.append(_dot_nt(y1[i], stack(state_b)) + y2[i])
        states[p] = states[p] * p_end[i] + _dot(state_b, m_low[i]) + c_wide[i]
    for p in range(n_pairs):
        state_ref[:, p * LANES:(p + 1) * LANES] = states[p]

    lnw, lnb, rk = lane_param(lnw_ref), lane_param(lnb_ref), lane_param(rk_ref)
    for i, (c, p) in enumerate(items):
        mean = head_sums(y[i]) / HEAD_DIM
        dev = y[i] - mean
        var = head_sums(dev * dev) / HEAD_DIM
        z = dev * lax.rsqrt(var + GN_EPS) * lnw[i] + lnb[i]
        bonus = head_sums(r[i] * k[i] * rk[i]) * v[i]
        o_ref[c * L:(c + 1) * L, p * LANES:(p + 1) * LANES] = (
            (z + bonus) * tile(g_ref, c, p).astype(F32)).astype(BF16)


def _rwkv_scan(r, lw, k, v, a, g, p, batch, seq, tb=256, lanes=8 * LANES):
    n = r.shape[0]
    tb = min(tb, seq)
    nt = seq // tb
    tile = pl.BlockSpec((tb, lanes), lambda b, h, t: (b * nt + t, h))
    lane_const = pl.BlockSpec((1, lanes), lambda b, h, t: (0, h))
    return pl.pallas_call(
        _rwkv_scan_kernel,
        grid=(batch, D_MODEL // lanes, nt),
        in_specs=[tile] * 6 + [lane_const] * 5,
        out_specs=tile,
        out_shape=jax.ShapeDtypeStruct((n, D_MODEL), BF16),
        scratch_shapes=[pltpu.VMEM((HEAD_DIM, lanes), F32)],
        compiler_params=_cparams("parallel", "parallel", "arbitrary"),
    )(r, lw, k, v, a, g, p['k_k'], p['k_a'], p['r_k'], p['ln_w'], p['ln_b'])


def _pad_to(t, axis, size):
    pad = [(0, 0)] * t.ndim
    pad[axis] = (0, size - t.shape[axis])
    return jnp.pad(t, pad)


def _row(t):
    return t.reshape(1, -1).astype(F32)


def _pair_tile(gain):
    return jnp.tile(gain.astype(F32), 2).reshape(1, LANES)


def kernel(x, mix_norm, ffn_norm, ffn_w_gate, ffn_w_up, ffn_w_down, hy_w_in, hy_f_bias, hy_q_gain, hy_k_gain, hy_pool_w, hy_pool_scale, hy_w_out, rw_mu, rw_w_r, rw_w_k, rw_w_v, rw_w0, rw_w1, rw_w2, rw_a0, rw_a1, rw_a2, rw_g1, rw_g2, rw_k_k, rw_k_a, rw_r_k, rw_ln_w, rw_ln_b, rw_w_o, rw_v0, rw_v1, rw_v2):
    batch, seq, d_model = x.shape
    assert d_model == D_MODEL and seq % 256 == 0
    depth = mix_norm.shape[0]
    n = batch * seq
    xs = x.reshape(n, d_model)
    v_first = None
    for layer in range(depth):
        gain = _row(mix_norm[layer])
        if layer % 2 == 0:
            e = layer // 2
            w_in = hy_w_in[e]
            w_qg = jnp.concatenate([w_in[:, :FOX_WIDTH], w_in[:, 3 * FOX_WIDTH:4 * FOX_WIDTH]], axis=1).astype(BF16)
            w_k = w_in[:, FOX_WIDTH:2 * FOX_WIDTH].astype(BF16)
            w_u = w_in[:, 4 * FOX_WIDTH + FOX_HEADS:].astype(BF16)
            wv_t = w_in[:, 2 * FOX_WIDTH:3 * FOX_WIDTH].T.astype(BF16)
            wf = _pad_to(w_in[:, 4 * FOX_WIDTH:4 * FOX_WIDTH + FOX_HEADS], 1, LANES).astype(BF16)
            f_bias = _pad_to(_row(hy_f_bias[e]), 1, LANES)
            qg, kaug, vt, u = _inproj_even(xs, gain, w_k, w_qg, w_u, wv_t, wf, f_bias,
                                           _pair_tile(hy_k_gain[e]), seq)
            attn = _fox_attn(qg, kaug, vt, _pair_tile(hy_q_gain[e]), batch, seq)
            mixed = _pool(u, hy_pool_w[e].astype(BF16), _row(hy_pool_scale[e]), seq)
            w_out = hy_w_out[e].astype(BF16)
            mixes = [(attn, w_out[:FOX_WIDTH]), (mixed, w_out[FOX_WIDTH:])]
        else:
            o = layer // 2
            p = dict(mu=rw_mu[o].astype(F32),
                     w_r=rw_w_r[o].astype(BF16), w_k=rw_w_k[o].astype(BF16), w_v=rw_w_v[o].astype(BF16),
                     w0=_row(rw_w0[o]), w1=_pad_to(rw_w1[o], 1, LANES).astype(BF16),
                     w2=_pad_to(rw_w2[o], 0, LANES).astype(BF16),
                     a0=_row(rw_a0[o]), a1=_pad_to(rw_a1[o], 1, LANES).astype(BF16),
                     a2=_pad_to(rw_a2[o], 0, LANES).astype(BF16),
                     g1=_pad_to(rw_g1[o], 1, 2 * LANES).astype(BF16),
                     g2=_pad_to(rw_g2[o], 0, 2 * LANES).astype(BF16),
                     k_k=_row(rw_k_k[o]), k_a=_row(rw_k_a[o]), r_k=_row(rw_r_k[o]),
                     ln_w=_row(rw_ln_w[o]), ln_b=_row(rw_ln_b[o]))
            if o > 0:
                p.update(v0=_row(rw_v0[o - 1]), v1=_pad_to(rw_v1[o - 1], 1, LANES).astype(BF16),
                         v2=_pad_to(rw_v2[o - 1], 0, LANES).astype(BF16))
            r, lw, k, v, a, g = _rwkv_proj(xs, gain, p, v_first if o > 0 else None, seq)
            if o == 0:
                v_first = v
            mixes = [(_rwkv_scan(r, lw, k, v, a, g, p, batch, seq), rw_w_o[o].astype(BF16))]
        xs = _proj_ffn(xs, mixes, _row(ffn_norm[layer]), ffn_w_gate[layer].astype(BF16),
                       ffn_w_up[layer].astype(BF16), ffn_w_down[layer].astype(BF16))
    return xs.reshape(batch, seq, d_model)
```

```python
import functools

import jax
import jax.numpy as jnp
from jax import lax
from jax.experimental import pallas as pl
from jax.experimental.pallas import tpu as pltpu

F32 = jnp.float32
BF16 = jnp.bfloat16

D_MODEL = 1024
HEAD_DIM = 64
LANES = 128
FOX_HEADS = 8
FOX_WIDTH = FOX_HEADS * HEAD_DIM
FOX_PAIRS = FOX_HEADS // 2
POOL_WINDOWS = (2, 4, 8, 16)
POOL_WIDTH = D_MODEL - FOX_WIDTH
POOL_HALO = 16
RMS_EPS = 1e-6
GN_EPS = 64e-5
CHUNK = 64
DECAY_SCALE = 0.6065306597126334
NEG_BIG = -1e30
LOG2E = 1.4426950408889634
ATTN_SUB = 512
ATTN_GROUP = 4
VT_ROWS = 80
VMEM_LIMIT = 56 * 1024 * 1024


def _cparams(*sem, flags=None):
    return pltpu.CompilerParams(dimension_semantics=sem, vmem_limit_bytes=VMEM_LIMIT, flags=flags)


def _rms(x, gain):
    ms = jnp.mean(x * x, axis=-1, keepdims=True)
    return x * lax.rsqrt(ms + RMS_EPS) * gain


def _dot(a, b):
    return jnp.dot(a, b, preferred_element_type=F32)


def _dot_nt(a, b):
    return lax.dot_general(a, b, (((1,), (1,)), ((), ())), preferred_element_type=F32)


def _dot_tn(a, b):
    return lax.dot_general(a, b, (((0,), (0,)), ((), ())), preferred_element_type=F32)


def _const_spec(shape):
    nd = len(shape)
    return pl.BlockSpec(shape, lambda *_: (0,) * nd, pipeline_mode=pl.Buffered(1))


def _pair_rms_scale(t, first):
    t2 = t * t
    s0 = jnp.sum(jnp.where(first, t2, 0.0), axis=-1, keepdims=True)
    s1 = jnp.sum(jnp.where(first, 0.0, t2), axis=-1, keepdims=True)
    return jnp.where(first, lax.rsqrt(s0 / HEAD_DIM + RMS_EPS), lax.rsqrt(s1 / HEAD_DIM + RMS_EPS))


def _split3_bf16(t):
    hi = t.astype(BF16)
    r = t - hi.astype(F32)
    mid = r.astype(BF16)
    return hi, mid, (r - mid.astype(F32)).astype(BF16)


def _inproj_even_kernel(x_ref, gain_ref, wk_ref, wqg_ref, wu_ref, wvt_ref, wf_ref, fb_ref, kg_ref, tri_ref,
                        qg_ref, kaug_ref, vt_ref, u_ref, carry_ref, *, tiles_per_seq):
    i = pl.program_id(0)
    hb = _rms(x_ref[...], gain_ref[...]).astype(BF16)
    z = _dot(hb, wf_ref[...]) + fb_ref[...]
    c = jnp.minimum(z, 0.0) - jnp.log1p(jnp.exp(-jnp.abs(z)))
    tm = c.shape[0]
    c3 = _dot(tri_ref[...], jnp.concatenate(_split3_bf16(c), axis=1))
    c = c3[:, :LANES] + c3[:, LANES:2 * LANES] + c3[:, 2 * LANES:]

    @pl.when(i % tiles_per_seq == 0)
    def _():
        carry_ref[...] = jnp.zeros_like(carry_ref)

    c = c + carry_ref[...]
    carry_ref[...] = c[tm - 1:tm, :]
    main = _dot(hb, wk_ref[...])
    lane = lax.broadcasted_iota(jnp.int32, (1, LANES), 1)
    first = lane < HEAD_DIM
    key_bias = c * (-LOG2E)
    for p in range(FOX_PAIRS):
        k = main[:, p * LANES:(p + 1) * LANES]
        kn = k * _pair_rms_scale(k, first) * kg_ref[...]
        for h in range(2):
            hi, mid, lo = (t.astype(F32) for t in _split3_bf16(key_bias[:, 2 * p + h:2 * p + h + 1]))
            spare = HEAD_DIM * (1 - h)
            pieces = jnp.where(lane == spare, hi, jnp.where(lane == spare + 1, mid,
                                                            jnp.where(lane == spare + 2, lo, 0.0)))
            own = first if h == 0 else jnp.logical_not(first)
            kaug_ref[:, (2 * p + h) * LANES:(2 * p + h + 1) * LANES] = jnp.where(own, kn, pieces).astype(BF16)
    qg_ref[...] = _dot(hb, wqg_ref[...]).astype(BF16)
    u_ref[...] = _dot(hb, wu_ref[...])
    v_t = _dot_nt(wvt_ref[...], hb)
    ones_row = jnp.where(lax.broadcasted_iota(jnp.int32, (VT_ROWS - HEAD_DIM, tm), 0) == 0, 1.0, 0.0).astype(BF16)
    for h in range(FOX_HEADS):
        vt_ref[h * VT_ROWS:h * VT_ROWS + HEAD_DIM, :] = v_t[h * HEAD_DIM:(h + 1) * HEAD_DIM, :].astype(BF16)
        vt_ref[h * VT_ROWS + HEAD_DIM:(h + 1) * VT_ROWS, :] = ones_row


def _inproj_even(x, gain, w_k, w_qg, w_u, wv_t, wf, f_bias, k_gain2, seq, tm=512):
    n = x.shape[0]
    tm = min(tm, seq)
    tri = jnp.tril(jnp.ones((tm, tm), BF16))
    consts = [gain, w_k, w_qg, w_u, wv_t, wf, f_bias, k_gain2, tri]
    return pl.pallas_call(
        functools.partial(_inproj_even_kernel, tiles_per_seq=seq // tm),
        grid=(n // tm,),
        in_specs=[pl.BlockSpec((tm, D_MODEL), lambda i: (i, 0))] + [_const_spec(c.shape) for c in consts],
        out_specs=[pl.BlockSpec((tm, 2 * FOX_WIDTH), lambda i: (i, 0)),
                   pl.BlockSpec((tm, 2 * FOX_WIDTH), lambda i: (i, 0)),
                   pl.BlockSpec((FOX_HEADS * VT_ROWS, tm), lambda i: (0, i)),
                   pl.BlockSpec((tm, POOL_WIDTH), lambda i: (i, 0))],
        out_shape=[jax.ShapeDtypeStruct((n, 2 * FOX_WIDTH), BF16),
                   jax.ShapeDtypeStruct((n, 2 * FOX_WIDTH), BF16),
                   jax.ShapeDtypeStruct((FOX_HEADS * VT_ROWS, n), BF16),
                   jax.ShapeDtypeStruct((n, POOL_WIDTH), F32)],
        scratch_shapes=[pltpu.VMEM((1, LANES), F32)],
        compiler_params=_cparams("arbitrary"),
    )(x, *consts)


def _fox_attn_kernel(q_ref, og_ref, k0_ref, k1_ref, vt_ref, qg_ref, o_ref, *, tq):
    qi = pl.program_id(2)
    lane = lax.broadcasted_iota(jnp.int32, (1, LANES), 1)
    first = lane < HEAD_DIM
    k_refs = (k0_ref, k1_ref)

    q = q_ref[...].astype(F32)
    qn = q * _pair_rms_scale(q, first) * (qg_ref[...] * (HEAD_DIM ** -0.5 * LOG2E))
    ones0 = jnp.where((lane >= HEAD_DIM) & (lane < HEAD_DIM + 3), 1.0, 0.0)
    ones1 = jnp.where(lane < 3, 1.0, 0.0)
    q_heads = (jnp.where(first, qn, ones0).astype(BF16), jnp.where(first, ones1, qn).astype(BF16))
    n_sub = tq // ATTN_SUB
    chains = [(u, h) for u in range(n_sub) for h in range(2)]
    q_sub = [q_heads[h][u * ATTN_SUB:(u + 1) * ATTN_SUB, :] for u, h in chains]
    key_row = lax.broadcasted_iota(jnp.int32, (tq, ATTN_SUB), 0)
    q_col = lax.broadcasted_iota(jnp.int32, (tq, ATTN_SUB), 1)

    def step(blocks, stats):
        rows = [pl.ds(pl.multiple_of(j * tq, tq), tq) for j, _ in blocks]
        s = []
        for (_, diagonal), r in zip(blocks, rows):
            sb = [_dot_nt(k_refs[h][r, :], q_sub[i]) for i, (u, h) in enumerate(chains)]
            if diagonal:
                sb = [jnp.where(key_row <= q_col + u * ATTN_SUB, x, NEG_BIG) for x, (u, h) in zip(sb, chains)]
            s.append(sb)
        m_new = [functools.reduce(jnp.maximum, [jnp.max(sb[i], axis=0, keepdims=True) for sb in s], st[0])
                 for i, st in enumerate(stats)]
        alpha = [jnp.exp2(st[0] - m) for st, m in zip(stats, m_new)]
        p = [[jnp.exp2(x - m) for x, m in zip(sb, m_new)] for sb in s]
        pv = [sum(_dot(vt_ref[h * VT_ROWS:(h + 1) * VT_ROWS, r], pb[i].astype(BF16)) for pb, r in zip(p, rows))
              for i, (u, h) in enumerate(chains)]
        return tuple((m, a * st[1] + y) for m, a, st, y in zip(m_new, alpha, stats, pv))

    def finish(stats):
        for u in range(n_sub):
            (_, a0), (_, a1) = stats[2 * u], stats[2 * u + 1]
            attn = jnp.concatenate([a0[:HEAD_DIM] / a0[HEAD_DIM:HEAD_DIM + 1],
                                    a1[:HEAD_DIM] / a1[HEAD_DIM:HEAD_DIM + 1]], axis=0).T
            rows = slice(u * ATTN_SUB, (u + 1) * ATTN_SUB)
            o_ref[rows, :] = (attn * jax.nn.sigmoid(og_ref[rows, :].astype(F32))).astype(BF16)

    init = tuple((jnp.full((1, ATTN_SUB), NEG_BIG, F32), jnp.zeros((VT_ROWS, ATTN_SUB), F32)) for _ in chains)
    stats = lax.fori_loop(0, qi // ATTN_GROUP,
                          lambda i, c: step([(ATTN_GROUP * i + g, False) for g in range(ATTN_GROUP)], c), init)
    for left in range(ATTN_GROUP):
        @pl.when(qi % ATTN_GROUP == left)
        def _(left=left):
            finish(step([(qi - left + g, False) for g in range(left)] + [(qi, True)], stats))


def _fox_attn(qg, kaug, vt, q_gain2, batch, seq, tq=512):
    n = qg.shape[0]
    tq = min(tq, seq)
    nq = seq // tq
    return pl.pallas_call(
        functools.partial(_fox_attn_kernel, tq=tq),
        grid=(batch, FOX_PAIRS, nq),
        in_specs=[pl.BlockSpec((tq, LANES), lambda b, p, i: (b * nq + i, p)),
                  pl.BlockSpec((tq, LANES), lambda b, p, i: (b * nq + i, FOX_PAIRS + p)),
                  pl.BlockSpec((seq, LANES), lambda b, p, i: (b, 2 * p)),
                  pl.BlockSpec((seq, LANES), lambda b, p, i: (b, 2 * p + 1)),
                  pl.BlockSpec((2 * VT_ROWS, seq), lambda b, p, i: (p, b)),
                  _const_spec(q_gain2.shape)],
        out_specs=pl.BlockSpec((tq, LANES), lambda b, p, i: (b * nq + i, p)),
        out_shape=jax.ShapeDtypeStruct((n, FOX_WIDTH), BF16),
        compiler_params=_cparams("parallel", "parallel", "arbitrary"),
    )(qg, qg, kaug, kaug, vt, q_gain2)


def _split_bf16(t):
    hi = t.astype(BF16)
    lo = (t - hi.astype(F32)).astype(BF16)
    return hi, lo


def _pool_kernel(u_ref, halo_ref, pw_ref, ps_ref, o_ref, *, tiles_per_seq):
    i = pl.program_id(0)
    tm = u_ref.shape[0]
    tile_in_seq = i % tiles_per_seq
    u = u_ref[...]
    halo = halo_ref[...] * (tile_in_seq != 0).astype(F32)
    row = lax.broadcasted_iota(jnp.int32, (tm, tm), 0)
    col = lax.broadcasted_iota(jnp.int32, (tm, tm), 1)
    hrow = lax.broadcasted_iota(jnp.int32, (tm, POOL_HALO), 0)
    hcol = lax.broadcasted_iota(jnp.int32, (tm, POOL_HALO), 1)
    pos = tile_in_seq * tm + lax.broadcasted_iota(jnp.int32, (tm, 1), 0)
    groups = range(len(POOL_WINDOWS))
    sl = [slice(g * LANES, (g + 1) * LANES) for g in groups]
    band = [jnp.where((col <= row) & (row - col < w), 1.0, 0.0).astype(BF16) for w in POOL_WINDOWS]
    hband = [jnp.where(hrow + POOL_HALO - hcol < w, 1.0, 0.0).astype(BF16) for w in POOL_WINDOWS]
    wsum2 = [_dot(band[g], jnp.concatenate(_split_bf16(u[:, sl[g]]), axis=1))
             + _dot(hband[g], jnp.concatenate(_split_bf16(halo[:, sl[g]]), axis=1)) for g in groups]
    pooled = [(wsum2[g][:, :LANES] + wsum2[g][:, LANES:]) / jnp.minimum(pos + 1, w).astype(F32) - u[:, sl[g]]
              for g, w in enumerate(POOL_WINDOWS)]
    mixed = [_dot(pooled[g].astype(BF16), pw_ref[g]) for g in groups]
    for g in groups:
        o_ref[:, sl[g]] = (mixed[g] * ps_ref[:, sl[g]]).astype(BF16)


def _pool(u, pool_w, pool_scale, seq, tm=512):
    n = u.shape[0]
    tm = min(tm, seq)
    hb = tm // POOL_HALO
    return pl.pallas_call(
        functools.partial(_pool_kernel, tiles_per_seq=seq // tm),
        grid=(n // tm,),
        in_specs=[pl.BlockSpec((tm, POOL_WIDTH), lambda i: (i, 0)),
                  pl.BlockSpec((POOL_HALO, POOL_WIDTH), lambda i: (jnp.maximum(i * hb - 1, 0), 0)),
                  _const_spec(pool_w.shape), _const_spec(pool_scale.shape)],
        out_specs=pl.BlockSpec((tm, POOL_WIDTH), lambda i: (i, 0)),
        out_shape=jax.ShapeDtypeStruct((n, POOL_WIDTH), BF16),
        compiler_params=_cparams("parallel"),
    )(u, u, pool_w, pool_scale)


def _proj_ffn_kernel(*refs, n_mix):
    x_ref = refs[0]
    mix_refs = refs[1:1 + 2 * n_mix]
    gain_ref, wg_ref, wu_ref, wd_ref, o_ref = refs[1 + 2 * n_mix:]
    x = x_ref[...]
    for m in range(n_mix):
        x = x + _dot(mix_refs[2 * m][...], mix_refs[2 * m + 1][...])
    hb = _rms(x, gain_ref[...]).astype(BF16)
    gate = _dot(hb, wg_ref[...])
    up = _dot(hb, wu_ref[...])
    act = (gate * jax.nn.sigmoid(gate) * up).astype(BF16)
    o_ref[...] = x + _dot(act, wd_ref[...])


def _proj_ffn(x, mixes, gain, w_gate, w_up, w_down, tm=512):
    n = x.shape[0]
    tm = min(tm, n)
    args, specs = [x], [pl.BlockSpec((tm, D_MODEL), lambda i: (i, 0))]
    for y, w in mixes:
        args += [y, w]
        specs += [pl.BlockSpec((tm, y.shape[1]), lambda i: (i, 0)), _const_spec(w.shape)]
    consts = [gain, w_gate, w_up, w_down]
    return pl.pallas_call(
        functools.partial(_proj_ffn_kernel, n_mix=len(mixes)),
        grid=(n // tm,),
        in_specs=specs + [_const_spec(c.shape) for c in consts],
        out_specs=pl.BlockSpec((tm, D_MODEL), lambda i: (i, 0)),
        out_shape=jax.ShapeDtypeStruct((n, D_MODEL), F32),
        compiler_params=_cparams("parallel"),
    )(*args, *consts)


def _rwkv_proj_kernel(*refs, tiles_per_seq, has_vmix):
    if has_vmix:
        (x_ref, halo_ref, gain_ref, mu_ref, wr_ref, wk_ref, wv_ref, w0_ref, w1_ref, w2_ref,
         a0_ref, a1_ref, a2_ref, g1_ref, g2_ref, vf_ref, v0_ref, v1_ref, v2_ref,
         r_ref, lw_ref, k_ref, v_ref, a_ref, g_ref) = refs
    else:
        (x_ref, halo_ref, gain_ref, mu_ref, wr_ref, wk_ref, wv_ref, w0_ref, w1_ref, w2_ref,
         a0_ref, a1_ref, a2_ref, g1_ref, g2_ref,
         r_ref, lw_ref, k_ref, v_ref, a_ref, g_ref) = refs
    i = pl.program_id(0)
    tm = x_ref.shape[0]
    gain = gain_ref[...]
    h = _rms(x_ref[...], gain)
    h_last = _rms(halo_ref[...], gain)[halo_ref.shape[0] - 1:, :]
    h_last = h_last * (i % tiles_per_seq != 0).astype(F32)
    rows = lax.broadcasted_iota(jnp.int32, (tm, 1), 0)
    h_prev = jnp.where(rows == 0, h_last, pltpu.roll(h, 1, axis=0))
    xx = h_prev - h

    def mix(idx):
        return (h + xx * mu_ref[idx:idx + 1, :]).astype(BF16)

    xr, xw, xk, xv, xa, xg = (mix(idx) for idx in range(6))
    r_ref[...] = _dot(xr, wr_ref[...]).astype(BF16)
    k_ref[...] = _dot(xk, wk_ref[...]).astype(BF16)
    v = _dot(xv, wv_ref[...])
    wl = w0_ref[...] + _dot(jnp.tanh(_dot(xw, w1_ref[...])).astype(BF16), w2_ref[...])
    lw_ref[...] = -DECAY_SCALE * jax.nn.sigmoid(wl)
    a_ref[...] = jax.nn.sigmoid(
        a0_ref[...] + _dot(_dot(xa, a1_ref[...]).astype(BF16), a2_ref[...])).astype(BF16)
    g_ref[...] = _dot(jax.nn.sigmoid(_dot(xg, g1_ref[...])).astype(BF16), g2_ref[...]).astype(BF16)
    if has_vmix:
        gate = jax.nn.sigmoid(v0_ref[...] + _dot(_dot(xv, v1_ref[...]).astype(BF16), v2_ref[...]))
        v = v + (vf_ref[...].astype(F32) - v) * gate
    v_ref[...] = v.astype(BF16)


def _rwkv_proj(x, gain, p, v_first, seq, tm=512):
    n = x.shape[0]
    tm = min(tm, seq)
    has_vmix = v_first is not None
    tile = pl.BlockSpec((tm, D_MODEL), lambda i: (i, 0))
    halo = pl.BlockSpec((8, D_MODEL), lambda i: (jnp.maximum(i * (tm // 8) - 1, 0), 0))
    consts = [gain, p['mu'], p['w_r'], p['w_k'], p['w_v'], p['w0'], p['w1'], p['w2'],
              p['a0'], p['a1'], p['a2'], p['g1'], p['g2']]
    args = [x, x] + consts
    specs = [tile, halo] + [_const_spec(c.shape) for c in consts]
    if has_vmix:
        extra = [p['v0'], p['v1'], p['v2']]
        args += [v_first] + extra
        specs += [tile] + [_const_spec(c.shape) for c in extra]
    out_dtypes = [BF16, F32, BF16, BF16, BF16, BF16]
    return pl.pallas_call(
        functools.partial(_rwkv_proj_kernel, tiles_per_seq=seq // tm, has_vmix=has_vmix),
        grid=(n // tm,),
        in_specs=specs,
        out_specs=[tile] * 6,
        out_shape=[jax.ShapeDtypeStruct((n, D_MODEL), dt) for dt in out_dtypes],
        compiler_params=_cparams("parallel"),
    )(*args)


def _rwkv_scan_kernel(r_ref, lw_ref, k_ref, v_ref, a_ref, g_ref, kk_ref, ka_ref, rk_ref, lnw_ref, lnb_ref,
                      o_ref, state_ref):
    t = pl.program_id(2)

    @pl.when(t == 0)
    def _():
        state_ref[...] = jnp.zeros_like(state_ref)

    L = CHUNK
    n_chunks = r_ref.shape[0] // L
    n_pairs = r_ref.shape[1] // LANES
    first = lax.broadcasted_iota(jnp.int32, (1, LANES), 1) < HEAD_DIM
    trow = lax.broadcasted_iota(jnp.int32, (L, LANES), 0)
    tcol = lax.broadcasted_iota(jnp.int32, (L, LANES), 1) & (HEAD_DIM - 1)
    strict = tcol < trow
    incl = tcol <= trow
    same16 = (trow >> 4) == (tcol >> 4)
    same32 = (trow >> 5) == (tcol >> 5)
    eye = jnp.where(trow == tcol, 1.0, 0.0)
    same_head = ((lax.broadcasted_iota(jnp.int32, (LANES, LANES), 0) >> 6)
                 == (lax.broadcasted_iota(jnp.int32, (LANES, LANES), 1) >> 6))
    tri = jnp.where(lax.broadcasted_iota(jnp.int32, (L, L), 1) <= lax.broadcasted_iota(jnp.int32, (L, L), 0),
                    1.0, 0.0).astype(BF16)

    def bf(z):
        return z.astype(BF16)

    def stack(zb):
        return jnp.concatenate([jnp.where(first, zb, 0), jnp.where(first, 0, zb)], axis=0)

    def wmm(x, y):
        return _dot(bf(x), stack(bf(y)))

    def head_sums(z):
        s0 = jnp.sum(jnp.where(first, z, 0.0), axis=-1, keepdims=True)
        s1 = jnp.sum(jnp.where(first, 0.0, z), axis=-1, keepdims=True)
        return jnp.where(first, s0, s1)

    def tile(ref, c, p):
        return ref[c * L:(c + 1) * L, p * LANES:(p + 1) * LANES]

    def process(items, states):
        def load(ref):
            return [tile(ref, c, p).astype(F32) for c, p in items]

        def lane_param(ref):
            return [ref[:, p * LANES:(p + 1) * LANES] for _, p in items]

        r, lw, kraw, v, asig = load(r_ref), load(lw_ref), load(k_ref), load(v_ref), load(a_ref)
        kk = [x * w for x, w in zip(kraw, lane_param(kk_ref))]
        kk = [x / jnp.maximum(jnp.sqrt(head_sums(x * x)), 1e-12) for x in kk]
        k = [x * (1.0 + (s - 1.0) * w) for x, s, w in zip(kraw, asig, lane_param(ka_ref))]
        b = [x * s for x, s in zip(kk, asig)]
        lw_hi = [bf(x) for x in lw]
        lw_lo = [bf(x - h.astype(F32)) for x, h in zip(lw, lw_hi)]
        cum = [_dot(tri, jnp.concatenate([h, l], axis=1)) for h, l in zip(lw_hi, lw_lo)]
        cum = [x[:, :LANES] + x[:, LANES:] for x in cum]
        cum_end = [x[L - 1:L, :] for x in cum]
        e_neg = [jnp.exp(-x) for x in cum]
        e_end = [jnp.exp(ce - x) for x, ce in zip(cum, cum_end)]
        rt = [x * jnp.exp(c) for x, c in zip(r, cum)]
        at_b = [bf(-x * jnp.exp(c - w)) for x, c, w in zip(kk, cum, lw)]
        bt = [bf(x * e) for x, e in zip(b, e_neg)]
        kt = [bf(x * e) for x, e in zip(k, e_neg)]
        bh = [bf(x * e) for x, e in zip(b, e_end)]
        kh = [bf(x * e) for x, e in zip(k, e_end)]
        v_b = [bf(x) for x in v]
        bonus = [head_sums(x * y * w) * z for x, y, w, z in zip(r, k, lane_param(rk_ref), v)]
        aa = [_dot_nt(jnp.concatenate([xa, bf(xr)], axis=0), jnp.concatenate([stack(xb), stack(xk)], axis=0))
              for xa, xr, xb, xk in zip(at_b, rt, bt, kt)]
        a_ab = [jnp.where(strict, x[:L, :LANES], 0.0) for x in aa]
        a_ak = [bf(jnp.where(strict, x[:L, LANES:], 0.0)) for x in aa]
        a_rb = [bf(jnp.where(incl, x[L:, :LANES], 0.0)) for x in aa]
        a_rk = [bf(jnp.where(incl, x[L:, LANES:], 0.0)) for x in aa]
        pw = [jnp.where(same16, x, 0.0) for x in a_ab]
        tinv = [eye + x for x in pw]
        pw = [wmm(x, x) for x in pw]
        for _ in range(2):
            both = [_dot(jnp.concatenate([bf(x), bf(p)], axis=0), stack(bf(p))) for x, p in zip(tinv, pw)]
            tinv = [x + y[:L] for x, y in zip(tinv, both)]
            pw = [y[L:] for y in both]
        tinv = [x + wmm(x, p) for x, p in zip(tinv, pw)]
        off = [jnp.where(same32 & ~same16, x, 0.0) for x in a_ab]
        tinv = [x + wmm(wmm(x, o), x) for x, o in zip(tinv, off)]
        off = [jnp.where(same32, 0.0, x) for x in a_ab]
        tinv = [x + wmm(wmm(x, o), x) for x, o in zip(tinv, off)]
        akv = [_dot(x, stack(y)) for x, y in zip(a_ak, v_b)]
        x12 = [_dot(bf(x), jnp.concatenate([stack(y), stack(bf(z))], axis=1))
               for x, y, z in zip(tinv, at_b, akv)]
        x1 = [bf(x[:, :LANES]) for x in x12]
        x2 = [bf(x[:, LANES:]) for x in x12]
        ab12 = [_dot(x, jnp.concatenate([stack(y), stack(z)], axis=1)) for x, y, z in zip(a_rb, x1, x2)]
        y1 = [bf(x + y[:, :LANES]) for x, y in zip(rt, ab12)]
        y2 = [y[:, LANES:] + _dot(x, stack(z)) for y, x, z in zip(ab12, a_rk, v_b)]
        m_low = [bf(jnp.where(same_head, _dot_tn(x, y), 0.0)) for x, y in zip(x1, bh)]
        c_full = [_dot_tn(jnp.concatenate([x, y], axis=0), jnp.concatenate([z, w], axis=0))
                  for x, y, z, w in zip(x2, v_b, bh, kh)]
        c_wide = [jnp.where(first, x[:L], x[L:]) for x in c_full]
        p_end = [jnp.exp(x) for x in cum_end]

        y = []
        for i, (c, p) in enumerate(items):
            state_b = bf(states[p])
            y.append(_dot_nt(y1[i], stack(state_b)) + y2[i])
            states[p] = states[p] * p_end[i] + _dot(state_b, m_low[i]) + c_wide[i]
        lnw, lnb = lane_param(lnw_ref), lane_param(lnb_ref)
        for i, (c, p) in enumerate(items):
            mean = head_sums(y[i]) / HEAD_DIM
            dev = y[i] - mean
            var = head_sums(dev * dev) / HEAD_DIM
            z = dev * lax.rsqrt(var + GN_EPS) * lnw[i] + lnb[i]
            o_ref[c * L:(c + 1) * L, p * LANES:(p + 1) * LANES] = (
                (z + bonus[i]) * tile(g_ref, c, p).astype(F32)).astype(BF16)
        return states

    states = [state_ref[:, p * LANES:(p + 1) * LANES] for p in range(n_pairs)]
    states = process([(c, p) for c in range(n_chunks) for p in range(n_pairs)], states)
    for p in range(n_pairs):
        state_ref[:, p * LANES:(p + 1) * LANES] = states[p]


def _rwkv_scan(r, lw, k, v, a, g, p, batch, seq, tb=256, lanes=8 * LANES):
    n = r.shape[0]
    tb = min(tb, seq)
    nt = seq // tb
    tile = pl.BlockSpec((tb, lanes), lambda b, h, t: (b * nt + t, h))
    lane_const = pl.BlockSpec((1, lanes), lambda b, h, t: (0, h))
    return pl.pallas_call(
        _rwkv_scan_kernel,
        grid=(batch, D_MODEL // lanes, nt),
        in_specs=[tile] * 6 + [lane_const] * 5,
        out_specs=tile,
        out_shape=jax.ShapeDtypeStruct((n, D_MODEL), BF16),
        scratch_shapes=[pltpu.VMEM((HEAD_DIM, lanes), F32)],
        compiler_params=_cparams("parallel", "parallel", "arbitrary"),
    )(r, lw, k, v, a, g, p['k_k'], p['k_a'], p['r_k'], p['ln_w'], p['ln_b'])


def _pad_to(t, axis, size):
    pad = [(0, 0)] * t.ndim
    pad[axis] = (0, size - t.shape[axis])
    return jnp.pad(t, pad)


def _row(t):
    return t.reshape(1, -1).astype(F32)


def _pair_tile(gain):
    return jnp.tile(gain.astype(F32), 2).reshape(1, LANES)


def kernel(x, mix_norm, ffn_norm, ffn_w_gate, ffn_w_up, ffn_w_down, hy_w_in, hy_f_bias, hy_q_gain, hy_k_gain, hy_pool_w, hy_pool_scale, hy_w_out, rw_mu, rw_w_r, rw_w_k, rw_w_v, rw_w0, rw_w1, rw_w2, rw_a0, rw_a1, rw_a2, rw_g1, rw_g2, rw_k_k, rw_k_a, rw_r_k, rw_ln_w, rw_ln_b, rw_w_o, rw_v0, rw_v1, rw_v2):
    batch, seq, d_model = x.shape
    assert d_model == D_MODEL and seq % 256 == 0
    depth = mix_norm.shape[0]
    n = batch * seq
    xs = x.reshape(n, d_model)
    v_first = None
    for layer in range(depth):
        gain = _row(mix_norm[layer])
        if layer % 2 == 0:
            e = layer // 2
            w_in = hy_w_in[e]
            w_qg = jnp.concatenate([w_in[:, :FOX_WIDTH], w_in[:, 3 * FOX_WIDTH:4 * FOX_WIDTH]], axis=1).astype(BF16)
            w_k = w_in[:, FOX_WIDTH:2 * FOX_WIDTH].astype(BF16)
            w_u = w_in[:, 4 * FOX_WIDTH + FOX_HEADS:].astype(BF16)
            wv_t = w_in[:, 2 * FOX_WIDTH:3 * FOX_WIDTH].T.astype(BF16)
            wf = _pad_to(w_in[:, 4 * FOX_WIDTH:4 * FOX_WIDTH + FOX_HEADS], 1, LANES).astype(BF16)
            f_bias = _pad_to(_row(hy_f_bias[e]), 1, LANES)
            qg, kaug, vt, u = _inproj_even(xs, gain, w_k, w_qg, w_u, wv_t, wf, f_bias,
                                           _pair_tile(hy_k_gain[e]), seq)
            attn = _fox_attn(qg, kaug, vt, _pair_tile(hy_q_gain[e]), batch, seq)
            mixed = _pool(u, hy_pool_w[e].astype(BF16), _row(hy_pool_scale[e]), seq)
            w_out = hy_w_out[e].astype(BF16)
            mixes = [(attn, w_out[:FOX_WIDTH]), (mixed, w_out[FOX_WIDTH:])]
        else:
            o = layer // 2
            p = dict(mu=rw_mu[o].astype(F32),
                     w_r=rw_w_r[o].astype(BF16), w_k=rw_w_k[o].astype(BF16), w_v=rw_w_v[o].astype(BF16),
                     w0=_row(rw_w0[o]), w1=_pad_to(rw_w1[o], 1, LANES).astype(BF16),
                     w2=_pad_to(rw_w2[o], 0, LANES).astype(BF16),
                     a0=_row(rw_a0[o]), a1=_pad_to(rw_a1[o], 1, LANES).astype(BF16),
                     a2=_pad_to(rw_a2[o], 0, LANES).astype(BF16),
                     g1=_pad_to(rw_g1[o], 1, 2 * LANES).astype(BF16),
                     g2=_pad_to(rw_g2[o], 0, 2 * LANES).astype(BF16),
                     k_k=_row(rw_k_k[o]), k_a=_row(rw_k_a[o]), r_k=_row(rw_r_k[o]),
                     ln_w=_row(rw_ln_w[o]), ln_b=_row(rw_ln_b[o]))
            if o > 0:
                p.update(v0=_row(rw_v0[o - 1]), v1=_pad_to(rw_v1[o - 1], 1, LANES).astype(BF16),
                         v2=_pad_to(rw_v2[o - 1], 0, LANES).astype(BF16))
            r, lw, k, v, a, g = _rwkv_proj(xs, gain, p, v_first if o > 0 else None, seq)
            if o == 0:
                v_first = v
            mixes = [(_rwkv_scan(r, lw, k, v, a, g, p, batch, seq), rw_w_o[o].astype(BF16))]
        xs = _proj_ffn(xs, mixes, _row(ffn_norm[layer]), ffn_w_gate[layer].astype(BF16),
                       ffn_w_up[layer].astype(BF16), ffn_w_down[layer].astype(BF16))
    return xs.reshape(batch, seq, d_model)
```

```python
import functools

import jax
import jax.numpy as jnp
from jax import lax
from jax.experimental import pallas as pl
from jax.experimental.pallas import tpu as pltpu

F32 = jnp.float32
BF16 = jnp.bfloat16

D_MODEL = 1024
HEAD_DIM = 64
LANES = 128
FOX_HEADS = 8
FOX_WIDTH = FOX_HEADS * HEAD_DIM
FOX_PAIRS = FOX_HEADS // 2
POOL_WINDOWS = (2, 4, 8, 16)
POOL_WIDTH = D_MODEL - FOX_WIDTH
POOL_HALO = 16
RMS_EPS = 1e-6
GN_EPS = 64e-5
CHUNK = 64
DECAY_SCALE = 0.6065306597126334
NEG_BIG = -1e30
LOG2E = 1.4426950408889634
ATTN_SUB = 512
ATTN_GROUP = 4
VT_ROWS = 80
VMEM_LIMIT = 56 * 1024 * 1024


def _cparams(*sem, flags=None):
    return pltpu.CompilerParams(dimension_semantics=sem, vmem_limit_bytes=VMEM_LIMIT, flags=flags)


def _rms(x, gain):
    ms = jnp.mean(x * x, axis=-1, keepdims=True)
    return x * lax.rsqrt(ms + RMS_EPS) * gain


def _dot(a, b):
    return jnp.dot(a, b, preferred_element_type=F32)


def _dot_nt(a, b):
    return lax.dot_general(a, b, (((1,), (1,)), ((), ())), preferred_element_type=F32)


def _dot_tn(a, b):
    return lax.dot_general(a, b, (((0,), (0,)), ((), ())), preferred_element_type=F32)


def _const_spec(shape):
    nd = len(shape)
    return pl.BlockSpec(shape, lambda *_: (0,) * nd, pipeline_mode=pl.Buffered(1))


def _pair_rms_scale(t, first):
    t2 = t * t
    s0 = jnp.sum(jnp.where(first, t2, 0.0), axis=-1, keepdims=True)
    s1 = jnp.sum(jnp.where(first, 0.0, t2), axis=-1, keepdims=True)
    return jnp.where(first, lax.rsqrt(s0 / HEAD_DIM + RMS_EPS), lax.rsqrt(s1 / HEAD_DIM + RMS_EPS))


def _split3_bf16(t):
    hi = t.astype(BF16)
    r = t - hi.astype(F32)
    mid = r.astype(BF16)
    return hi, mid, (r - mid.astype(F32)).astype(BF16)


def _inproj_even_kernel(x_ref, gain_ref, wk_ref, wqg_ref, wu_ref, wvt_ref, wf_ref, fb_ref, kg_ref, tri_ref,
                        qg_ref, kaug_ref, vt_ref, u_ref, carry_ref, *, tiles_per_seq):
    i = pl.program_id(0)
    hb = _rms(x_ref[...], gain_ref[...]).astype(BF16)
    z = _dot(hb, wf_ref[...]) + fb_ref[...]
    c = jnp.minimum(z, 0.0) - jnp.log1p(jnp.exp(-jnp.abs(z)))
    tm = c.shape[0]
    c3 = _dot(tri_ref[...], jnp.concatenate(_split3_bf16(c), axis=1))
    c = c3[:, :LANES] + c3[:, LANES:2 * LANES] + c3[:, 2 * LANES:]

    @pl.when(i % tiles_per_seq == 0)
    def _():
        carry_ref[...] = jnp.zeros_like(carry_ref)

    c = c + carry_ref[...]
    carry_ref[...] = c[tm - 1:tm, :]
    main = _dot(hb, wk_ref[...])
    lane = lax.broadcasted_iota(jnp.int32, (1, LANES), 1)
    first = lane < HEAD_DIM
    key_bias = c * (-LOG2E)
    for p in range(FOX_PAIRS):
        k = main[:, p * LANES:(p + 1) * LANES]
        kn = k * _pair_rms_scale(k, first) * kg_ref[...]
        for h in range(2):
            hi, mid, lo = (t.astype(F32) for t in _split3_bf16(key_bias[:, 2 * p + h:2 * p + h + 1]))
            spare = HEAD_DIM * (1 - h)
            pieces = jnp.where(lane == spare, hi, jnp.where(lane == spare + 1, mid,
                                                            jnp.where(lane == spare + 2, lo, 0.0)))
            own = first if h == 0 else jnp.logical_not(first)
            kaug_ref[:, (2 * p + h) * LANES:(2 * p + h + 1) * LANES] = jnp.where(own, kn, pieces).astype(BF16)
    qg_ref[...] = _dot(hb, wqg_ref[...]).astype(BF16)
    u_ref[...] = _dot(hb, wu_ref[...])
    v_t = _dot_nt(wvt_ref[...], hb)
    ones_row = jnp.where(lax.broadcasted_iota(jnp.int32, (VT_ROWS - HEAD_DIM, tm), 0) == 0, 1.0, 0.0).astype(BF16)
    for h in range(FOX_HEADS):
        vt_ref[h * VT_ROWS:h * VT_ROWS + HEAD_DIM, :] = v_t[h * HEAD_DIM:(h + 1) * HEAD_DIM, :].astype(BF16)
        vt_ref[h * VT_ROWS + HEAD_DIM:(h + 1) * VT_ROWS, :] = ones_row


def _inproj_even(x, gain, w_k, w_qg, w_u, wv_t, wf, f_bias, k_gain2, seq, tm=512):
    n = x.shape[0]
    tm = min(tm, seq)
    tri = jnp.tril(jnp.ones((tm, tm), BF16))
    consts = [gain, w_k, w_qg, w_u, wv_t, wf, f_bias, k_gain2, tri]
    return pl.pallas_call(
        functools.partial(_inproj_even_kernel, tiles_per_seq=seq // tm),
        grid=(n // tm,),
        in_specs=[pl.BlockSpec((tm, D_MODEL), lambda i: (i, 0))] + [_const_spec(c.shape) for c in consts],
        out_specs=[pl.BlockSpec((tm, 2 * FOX_WIDTH), lambda i: (i, 0)),
                   pl.BlockSpec((tm, 2 * FOX_WIDTH), lambda i: (i, 0)),
                   pl.BlockSpec((FOX_HEADS * VT_ROWS, tm), lambda i: (0, i)),
                   pl.BlockSpec((tm, POOL_WIDTH), lambda i: (i, 0))],
        out_shape=[jax.ShapeDtypeStruct((n, 2 * FOX_WIDTH), BF16),
                   jax.ShapeDtypeStruct((n, 2 * FOX_WIDTH), BF16),
                   jax.ShapeDtypeStruct((FOX_HEADS * VT_ROWS, n), BF16),
                   jax.ShapeDtypeStruct((n, POOL_WIDTH), F32)],
        scratch_shapes=[pltpu.VMEM((1, LANES), F32)],
        compiler_params=_cparams("arbitrary"),
    )(x, *consts)


def _fox_attn_kernel(q_ref, og_ref, k0_ref, k1_ref, vt_ref, qg_ref, o_ref, *, tq):
    qi = pl.program_id(2)
    lane = lax.broadcasted_iota(jnp.int32, (1, LANES), 1)
    first = lane < HEAD_DIM
    k_refs = (k0_ref, k1_ref)

    q = q_ref[...].astype(F32)
    qn = q * _pair_rms_scale(q, first) * (qg_ref[...] * (HEAD_DIM ** -0.5 * LOG2E))
    ones0 = jnp.where((lane >= HEAD_DIM) & (lane < HEAD_DIM + 3), 1.0, 0.0)
    ones1 = jnp.where(lane < 3, 1.0, 0.0)
    q_heads = (jnp.where(first, qn, ones0).astype(BF16), jnp.where(first, ones1, qn).astype(BF16))
    n_sub = tq // ATTN_SUB
    chains = [(u, h) for u in range(n_sub) for h in range(2)]
    q_sub = [q_heads[h][u * ATTN_SUB:(u + 1) * ATTN_SUB, :] for u, h in chains]
    key_row = lax.broadcasted_iota(jnp.int32, (tq, ATTN_SUB), 0)
    q_col = lax.broadcasted_iota(jnp.int32, (tq, ATTN_SUB), 1)

    def step(blocks, stats):
        rows = [pl.ds(pl.multiple_of(j * tq, tq), tq) for j, _ in blocks]
        s = []
        for (_, diagonal), r in zip(blocks, rows):
            sb = [_dot_nt(k_refs[h][r, :], q_sub[i]) for i, (u, h) in enumerate(chains)]
            if diagonal:
                sb = [jnp.where(key_row <= q_col + u * ATTN_SUB, x, NEG_BIG) for x, (u, h) in zip(sb, chains)]
            s.append(sb)
        m_new = [functools.reduce(jnp.maximum, [jnp.max(sb[i], axis=0, keepdims=True) for sb in s], st[0])
                 for i, st in enumerate(stats)]
        alpha = [jnp.exp2(st[0] - m) for st, m in zip(stats, m_new)]
        p = [[jnp.exp2(x - m) for x, m in zip(sb, m_new)] for sb in s]
        pv = [sum(_dot(vt_ref[h * VT_ROWS:(h + 1) * VT_ROWS, r], pb[i].astype(BF16)) for pb, r in zip(p, rows))
              for i, (u, h) in enumerate(chains)]
        return tuple((m, a * st[1] + y) for m, a, st, y in zip(m_new, alpha, stats, pv))

    def finish(stats):
        for u in range(n_sub):
            (_, a0), (_, a1) = stats[2 * u], stats[2 * u + 1]
            attn = jnp.concatenate([a0[:HEAD_DIM] / a0[HEAD_DIM:HEAD_DIM + 1],
                                    a1[:HEAD_DIM] / a1[HEAD_DIM:HEAD_DIM + 1]], axis=0).T
            rows = slice(u * ATTN_SUB, (u + 1) * ATTN_SUB)
            o_ref[rows, :] = (attn * jax.nn.sigmoid(og_ref[rows, :].astype(F32))).astype(BF16)

    init = tuple((jnp.full((1, ATTN_SUB), NEG_BIG, F32), jnp.zeros((VT_ROWS, ATTN_SUB), F32)) for _ in chains)
    stats = lax.fori_loop(0, qi // ATTN_GROUP,
                          lambda i, c: step([(ATTN_GROUP * i + g, False) for g in range(ATTN_GROUP)], c), init)
    for left in range(ATTN_GROUP):
        @pl.when(qi % ATTN_GROUP == left)
        def _(left=left):
            finish(step([(qi - left + g, False) for g in range(left)] + [(qi, True)], stats))


def _fox_attn(qg, kaug, vt, q_gain2, batch, seq, tq=512):
    n = qg.shape[0]
    tq = min(tq, seq)
    nq = seq // tq
    return pl.pallas_call(
        functools.partial(_fox_attn_kernel, tq=tq),
        grid=(batch, FOX_PAIRS, nq),
        in_specs=[pl.BlockSpec((tq, LANES), lambda b, p, i: (b * nq + i, p)),
                  pl.BlockSpec((tq, LANES), lambda b, p, i: (b * nq + i, FOX_PAIRS + p)),
                  pl.BlockSpec((seq, LANES), lambda b, p, i: (b, 2 * p)),
                  pl.BlockSpec((seq, LANES), lambda b, p, i: (b, 2 * p + 1)),
                  pl.BlockSpec((2 * VT_ROWS, seq), lambda b, p, i: (p, b)),
                  _const_spec(q_gain2.shape)],
        out_specs=pl.BlockSpec((tq, LANES), lambda b, p, i: (b * nq + i, p)),
        out_shape=jax.ShapeDtypeStruct((n, FOX_WIDTH), BF16),
        compiler_params=_cparams("parallel", "parallel", "arbitrary"),
    )(qg, qg, kaug, kaug, vt, q_gain2)


def _split_bf16(t):
    hi = t.astype(BF16)
    lo = (t - hi.astype(F32)).astype(BF16)
    return hi, lo


def _pool_kernel(u_ref, halo_ref, pw_ref, ps_ref, o_ref, *, tiles_per_seq):
    i = pl.program_id(0)
    tm = u_ref.shape[0]
    tile_in_seq = i % tiles_per_seq
    u = u_ref[...]
    halo = halo_ref[...] * (tile_in_seq != 0).astype(F32)
    row = lax.broadcasted_iota(jnp.int32, (tm, tm), 0)
    col = lax.broadcasted_iota(jnp.int32, (tm, tm), 1)
    hrow = lax.broadcasted_iota(jnp.int32, (tm, POOL_HALO), 0)
    hcol = lax.broadcasted_iota(jnp.int32, (tm, POOL_HALO), 1)
    pos = tile_in_seq * tm + lax.broadcasted_iota(jnp.int32, (tm, 1), 0)
    groups = range(len(POOL_WINDOWS))
    sl = [slice(g * LANES, (g + 1) * LANES) for g in groups]
    band = [jnp.where((col <= row) & (row - col < w), 1.0, 0.0).astype(BF16) for w in POOL_WINDOWS]
    hband = [jnp.where(hrow + POOL_HALO - hcol < w, 1.0, 0.0).astype(BF16) for w in POOL_WINDOWS]
    wsum2 = [_dot(band[g], jnp.concatenate(_split_bf16(u[:, sl[g]]), axis=1))
             + _dot(hband[g], jnp.concatenate(_split_bf16(halo[:, sl[g]]), axis=1)) for g in groups]
    pooled = [(wsum2[g][:, :LANES] + wsum2[g][:, LANES:]) / jnp.minimum(pos + 1, w).astype(F32) - u[:, sl[g]]
              for g, w in enumerate(POOL_WINDOWS)]
    mixed = [_dot(pooled[g].astype(BF16), pw_ref[g]) for g in groups]
    for g in groups:
        o_ref[:, sl[g]] = (mixed[g] * ps_ref[:, sl[g]]).astype(BF16)


def _pool(u, pool_w, pool_scale, seq, tm=512):
    n = u.shape[0]
    tm = min(tm, seq)
    hb = tm // POOL_HALO
    return pl.pallas_call(
        functools.partial(_pool_kernel, tiles_per_seq=seq // tm),
        grid=(n // tm,),
        in_specs=[pl.BlockSpec((tm, POOL_WIDTH), lambda i: (i, 0)),
                  pl.BlockSpec((POOL_HALO, POOL_WIDTH), lambda i: (jnp.maximum(i * hb - 1, 0), 0)),
                  _const_spec(pool_w.shape), _const_spec(pool_scale.shape)],
        out_specs=pl.BlockSpec((tm, POOL_WIDTH), lambda i: (i, 0)),
        out_shape=jax.ShapeDtypeStruct((n, POOL_WIDTH), BF16),
        compiler_params=_cparams("parallel"),
    )(u, u, pool_w, pool_scale)


def _proj_ffn_kernel(*refs, n_mix):
    x_ref = refs[0]
    mix_refs = refs[1:1 + 2 * n_mix]
    gain_ref, wg_ref, wu_ref, wd_ref, o_ref = refs[1 + 2 * n_mix:]
    x = x_ref[...]
    for m in range(n_mix):
        x = x + _dot(mix_refs[2 * m][...], mix_refs[2 * m + 1][...])
    hb = _rms(x, gain_ref[...]).astype(BF16)
    gate = _dot(hb, wg_ref[...])
    up = _dot(hb, wu_ref[...])
    act = (gate * jax.nn.sigmoid(gate) * up).astype(BF16)
    o_ref[...] = x + _dot(act, wd_ref[...])


def _proj_ffn(x, mixes, gain, w_gate, w_up, w_down, tm=512):
    n = x.shape[0]
    tm = min(tm, n)
    args, specs = [x], [pl.BlockSpec((tm, D_MODEL), lambda i: (i, 0))]
    for y, w in mixes:
        args += [y, w]
        specs += [pl.BlockSpec((tm, y.shape[1]), lambda i: (i, 0)), _const_spec(w.shape)]
    consts = [gain, w_gate, w_up, w_down]
    return pl.pallas_call(
        functools.partial(_proj_ffn_kernel, n_mix=len(mixes)),
        grid=(n // tm,),
        in_specs=specs + [_const_spec(c.shape) for c in consts],
        out_specs=pl.BlockSpec((tm, D_MODEL), lambda i: (i, 0)),
        out_shape=jax.ShapeDtypeStruct((n, D_MODEL), F32),
        compiler_params=_cparams("parallel"),
    )(*args, *consts)


def _rwkv_proj_kernel(*refs, tiles_per_seq, has_vmix):
    if has_vmix:
        (x_ref, halo_ref, gain_ref, mu_ref, wr_ref, wk_ref, wv_ref, w0_ref, w1_ref, w2_ref,
         a0_ref, a1_ref, a2_ref, g1_ref, g2_ref, vf_ref, v0_ref, v1_ref, v2_ref,
         r_ref, lw_ref, k_ref, v_ref, a_ref, g_ref) = refs
    else:
        (x_ref, halo_ref, gain_ref, mu_ref, wr_ref, wk_ref, wv_ref, w0_ref, w1_ref, w2_ref,
         a0_ref, a1_ref, a2_ref, g1_ref, g2_ref,
         r_ref, lw_ref, k_ref, v_ref, a_ref, g_ref) = refs
    i = pl.program_id(0)
    tm = x_ref.shape[0]
    gain = gain_ref[...]
    h = _rms(x_ref[...], gain)
    h_last = _rms(halo_ref[...], gain)[halo_ref.shape[0] - 1:, :]
    h_last = h_last * (i % tiles_per_seq != 0).astype(F32)
    rows = lax.broadcasted_iota(jnp.int32, (tm, 1), 0)
    h_prev = jnp.where(rows == 0, h_last, pltpu.roll(h, 1, axis=0))
    xx = h_prev - h

    def mix(idx):
        return (h + xx * mu_ref[idx:idx + 1, :]).astype(BF16)

    xr, xw, xk, xv, xa, xg = (mix(idx) for idx in range(6))
    r_ref[...] = _dot(xr, wr_ref[...]).astype(BF16)
    k_ref[...] = _dot(xk, wk_ref[...]).astype(BF16)
    v = _dot(xv, wv_ref[...])
    wl = w0_ref[...] + _dot(jnp.tanh(_dot(xw, w1_ref[...])).astype(BF16), w2_ref[...])
    lw_ref[...] = -DECAY_SCALE * jax.nn.sigmoid(wl)
    a_ref[...] = jax.nn.sigmoid(
        a0_ref[...] + _dot(_dot(xa, a1_ref[...]).astype(BF16), a2_ref[...])).astype(BF16)
    g_ref[...] = _dot(jax.nn.sigmoid(_dot(xg, g1_ref[...])).astype(BF16), g2_ref[...]).astype(BF16)
    if has_vmix:
        gate = jax.nn.sigmoid(v0_ref[...] + _dot(_dot(xv, v1_ref[...]).astype(BF16), v2_ref[...]))
        v = v + (vf_ref[...].astype(F32) - v) * gate
    v_ref[...] = v.astype(BF16)


def _rwkv_proj(x, gain, p, v_first, seq, tm=512):
    n = x.shape[0]
    tm = min(tm, seq)
    has_vmix = v_first is not None
    tile = pl.BlockSpec((tm, D_MODEL), lambda i: (i, 0))
    halo = pl.BlockSpec((8, D_MODEL), lambda i: (jnp.maximum(i * (tm // 8) - 1, 0), 0))
    consts = [gain, p['mu'], p['w_r'], p['w_k'], p['w_v'], p['w0'], p['w1'], p['w2'],
              p['a0'], p['a1'], p['a2'], p['g1'], p['g2']]
    args = [x, x] + consts
    specs = [tile, halo] + [_const_spec(c.shape) for c in consts]
    if has_vmix:
        extra = [p['v0'], p['v1'], p['v2']]
        args += [v_first] + extra
        specs += [tile] + [_const_spec(c.shape) for c in extra]
    out_dtypes = [BF16, F32, BF16, BF16, BF16, BF16]
    return pl.pallas_call(
        functools.partial(_rwkv_proj_kernel, tiles_per_seq=seq // tm, has_vmix=has_vmix),
        grid=(n // tm,),
        in_specs=specs,
        out_specs=[tile] * 6,
        out_shape=[jax.ShapeDtypeStruct((n, D_MODEL), dt) for dt in out_dtypes],
        compiler_params=_cparams("parallel"),
    )(*args)


def _rwkv_scan_kernel(r_ref, lw_ref, k_ref, v_ref, a_ref, g_ref, kk_ref, ka_ref, rk_ref, lnw_ref, lnb_ref,
                      o_ref, state_ref):
    t = pl.program_id(2)

    @pl.when(t == 0)
    def _():
        state_ref[...] = jnp.zeros_like(state_ref)

    L = CHUNK
    n_chunks = r_ref.shape[0] // L
    n_pairs = r_ref.shape[1] // LANES
    first = lax.broadcasted_iota(jnp.int32, (1, LANES), 1) < HEAD_DIM
    trow = lax.broadcasted_iota(jnp.int32, (L, LANES), 0)
    tcol = lax.broadcasted_iota(jnp.int32, (L, LANES), 1) & (HEAD_DIM - 1)
    strict = tcol < trow
    incl = tcol <= trow
    same16 = (trow >> 4) == (tcol >> 4)
    same32 = (trow >> 5) == (tcol >> 5)
    eye = jnp.where(trow == tcol, 1.0, 0.0)
    same_head = ((lax.broadcasted_iota(jnp.int32, (LANES, LANES), 0) >> 6)
                 == (lax.broadcasted_iota(jnp.int32, (LANES, LANES), 1) >> 6))
    tri = jnp.where(lax.broadcasted_iota(jnp.int32, (L, L), 1) <= lax.broadcasted_iota(jnp.int32, (L, L), 0),
                    1.0, 0.0).astype(BF16)

    def bf(z):
        return z.astype(BF16)

    def stack(zb):
        return jnp.concatenate([jnp.where(first, zb, 0), jnp.where(first, 0, zb)], axis=0)

    def wmm(x, y):
        return _dot(bf(x), stack(bf(y)))

    def head_sums(z):
        s0 = jnp.sum(jnp.where(first, z, 0.0), axis=-1, keepdims=True)
        s1 = jnp.sum(jnp.where(first, 0.0, z), axis=-1, keepdims=True)
        return jnp.where(first, s0, s1)

    def tile(ref, c, p):
        return ref[c * L:(c + 1) * L, p * LANES:(p + 1) * LANES]

    def process(items, states):
        def load(ref):
            return [tile(ref, c, p).astype(F32) for c, p in items]

        def lane_param(ref):
            return [ref[:, p * LANES:(p + 1) * LANES] for _, p in items]

        r, lw, kraw, v, asig = load(r_ref), load(lw_ref), load(k_ref), load(v_ref), load(a_ref)
        kk = [x * w for x, w in zip(kraw, lane_param(kk_ref))]
        kk = [x / jnp.maximum(jnp.sqrt(head_sums(x * x)), 1e-12) for x in kk]
        k = [x * (1.0 + (s - 1.0) * w) for x, s, w in zip(kraw, asig, lane_param(ka_ref))]
        b = [x * s for x, s in zip(kk, asig)]
        lw_hi = [bf(x) for x in lw]
        lw_lo = [bf(x - h.astype(F32)) for x, h in zip(lw, lw_hi)]
        cum = [_dot(tri, jnp.concatenate([h, l], axis=1)) for h, l in zip(lw_hi, lw_lo)]
        cum = [x[:, :LANES] + x[:, LANES:] for x in cum]
        cum_end = [x[L - 1:L, :] for x in cum]
        e_neg = [jnp.exp(-x) for x in cum]
        e_end = [jnp.exp(ce - x) for x, ce in zip(cum, cum_end)]
        rt = [x * jnp.exp(c) for x, c in zip(r, cum)]
        at_b = [bf(-x * jnp.exp(c - w)) for x, c, w in zip(kk, cum, lw)]
        bt = [bf(x * e) for x, e in zip(b, e_neg)]
        kt = [bf(x * e) for x, e in zip(k, e_neg)]
        bh = [bf(x * e) for x, e in zip(b, e_end)]
        kh = [bf(x * e) for x, e in zip(k, e_end)]
        v_b = [bf(x) for x in v]
        bonus = [head_sums(x * y * w) * z for x, y, w, z in zip(r, k, lane_param(rk_ref), v)]
        aa = [_dot_nt(jnp.concatenate([xa, bf(xr)], axis=0), jnp.concatenate([stack(xb), stack(xk)], axis=0))
              for xa, xr, xb, xk in zip(at_b, rt, bt, kt)]
        a_ab = [jnp.where(strict, x[:L, :LANES], 0.0) for x in aa]
        a_ak = [bf(jnp.where(strict, x[:L, LANES:], 0.0)) for x in aa]
        a_rb = [bf(jnp.where(incl, x[L:, :LANES], 0.0)) for x in aa]
        a_rk = [bf(jnp.where(incl, x[L:, LANES:], 0.0)) for x in aa]
        pw = [jnp.where(same16, x, 0.0) for x in a_ab]
        tinv = [eye + x for x in pw]
        pw = [wmm(x, x) for x in pw]
        for _ in range(2):
            both = [_dot(jnp.concatenate([bf(x), bf(p)], axis=0), stack(bf(p))) for x, p in zip(tinv, pw)]
            tinv = [x + y[:L] for x, y in zip(tinv, both)]
            pw = [y[L:] for y in both]
        tinv = [x + wmm(x, p) for x, p in zip(tinv, pw)]
        off = [jnp.where(same32 & ~same16, x, 0.0) for x in a_ab]
        tinv = [x + wmm(wmm(x, o), x) for x, o in zip(tinv, off)]
        off = [jnp.where(same32, 0.0, x) for x in a_ab]
        tinv = [x + wmm(wmm(x, o), x) for x, o in zip(tinv, off)]
        akv = [_dot(x, stack(y)) for x, y in zip(a_ak, v_b)]
        x12 = [_dot(bf(x), jnp.concatenate([stack(y), stack(bf(z))], axis=1))
               for x, y, z in zip(tinv, at_b, akv)]
        x1 = [bf(x[:, :LANES]) for x in x12]
        x2 = [bf(x[:, LANES:]) for x in x12]
        ab12 = [_dot(x, jnp.concatenate([stack(y), stack(z)], axis=1)) for x, y, z in zip(a_rb, x1, x2)]
        y1 = [bf(x + y[:, :LANES]) for x, y in zip(rt, ab12)]
        y2 = [y[:, LANES:] + _dot(x, stack(z)) for y, x, z in zip(ab12, a_rk, v_b)]
        m_low = [bf(jnp.where(same_head, _dot_tn(x, y), 0.0)) for x, y in zip(x1, bh)]
        c_full = [_dot_tn(jnp.concatenate([x, y], axis=0), jnp.concatenate([z, w], axis=0))
                  for x, y, z, w in zip(x2, v_b, bh, kh)]
        c_wide = [jnp.where(first, x[:L], x[L:]) for x in c_full]
        p_end = [jnp.exp(x) for x in cum_end]

        y = []
        for i, (c, p) in enumerate(items):
            state_b = bf(states[p])
            y.append(_dot_nt(y1[i], stack(state_b)) + y2[i])
            states[p] = states[p] * p_end[i] + _dot(state_b, m_low[i]) + c_wide[i]
        lnw, lnb = lane_param(lnw_ref), lane_param(lnb_ref)
        for i, (c, p) in enumerate(items):
            mean = head_sums(y[i]) / HEAD_DIM
            dev = y[i] - mean
            var = head_sums(dev * dev) / HEAD_DIM
            z = dev * lax.rsqrt(var + GN_EPS) * lnw[i] + lnb[i]
            o_ref[c * L:(c + 1) * L, p * LANES:(p + 1) * LANES] = (
                (z + bonus[i]) * tile(g_ref, c, p).astype(F32)).astype(BF16)
        return states

    states = [state_ref[:, p * LANES:(p + 1) * LANES] for p in range(n_pairs)]
    states = process([(c, p) for c in range(n_chunks) for p in range(n_pairs)], states)
    for p in range(n_pairs):
        state_ref[:, p * LANES:(p + 1) * LANES] = states[p]


def _rwkv_scan(r, lw, k, v, a, g, p, batch, seq, tb=512, lanes=8 * LANES):
    n = r.shape[0]
    tb = min(tb, seq)
    nt = seq // tb
    tile = pl.BlockSpec((tb, lanes), lambda b, h, t: (b * nt + t, h))
    lane_const = pl.BlockSpec((1, lanes), lambda b, h, t: (0, h))
    return pl.pallas_call(
        _rwkv_scan_kernel,
        grid=(batch, D_MODEL // lanes, nt),
        in_specs=[tile] * 6 + [lane_const] * 5,
        out_specs=tile,
        out_shape=jax.ShapeDtypeStruct((n, D_MODEL), BF16),
        scratch_shapes=[pltpu.VMEM((HEAD_DIM, lanes), F32)],
        compiler_params=_cparams("parallel", "parallel", "arbitrary"),
    )(r, lw, k, v, a, g, p['k_k'], p['k_a'], p['r_k'], p['ln_w'], p['ln_b'])


def _pad_to(t, axis, size):
    pad = [(0, 0)] * t.ndim
    pad[axis] = (0, size - t.shape[axis])
    return jnp.pad(t, pad)


def _row(t):
    return t.reshape(1, -1).astype(F32)


def _pair_tile(gain):
    return jnp.tile(gain.astype(F32), 2).reshape(1, LANES)


def kernel(x, mix_norm, ffn_norm, ffn_w_gate, ffn_w_up, ffn_w_down, hy_w_in, hy_f_bias, hy_q_gain, hy_k_gain, hy_pool_w, hy_pool_scale, hy_w_out, rw_mu, rw_w_r, rw_w_k, rw_w_v, rw_w0, rw_w1, rw_w2, rw_a0, rw_a1, rw_a2, rw_g1, rw_g2, rw_k_k, rw_k_a, rw_r_k, rw_ln_w, rw_ln_b, rw_w_o, rw_v0, rw_v1, rw_v2):
    batch, seq, d_model = x.shape
    assert d_model == D_MODEL and seq % 256 == 0
    depth = mix_norm.shape[0]
    n = batch * seq
    xs = x.reshape(n, d_model)
    v_first = None
    for layer in range(depth):
        gain = _row(mix_norm[layer])
        if layer % 2 == 0:
            e = layer // 2
            w_in = hy_w_in[e]
            w_qg = jnp.concatenate([w_in[:, :FOX_WIDTH], w_in[:, 3 * FOX_WIDTH:4 * FOX_WIDTH]], axis=1).astype(BF16)
            w_k = w_in[:, FOX_WIDTH:2 * FOX_WIDTH].astype(BF16)
            w_u = w_in[:, 4 * FOX_WIDTH + FOX_HEADS:].astype(BF16)
            wv_t = w_in[:, 2 * FOX_WIDTH:3 * FOX_WIDTH].T.astype(BF16)
            wf = _pad_to(w_in[:, 4 * FOX_WIDTH:4 * FOX_WIDTH + FOX_HEADS], 1, LANES).astype(BF16)
            f_bias = _pad_to(_row(hy_f_bias[e]), 1, LANES)
            qg, kaug, vt, u = _inproj_even(xs, gain, w_k, w_qg, w_u, wv_t, wf, f_bias,
                                           _pair_tile(hy_k_gain[e]), seq)
            attn = _fox_attn(qg, kaug, vt, _pair_tile(hy_q_gain[e]), batch, seq)
            mixed = _pool(u, hy_pool_w[e].astype(BF16), _row(hy_pool_scale[e]), seq)
            w_out = hy_w_out[e].astype(BF16)
            mixes = [(attn, w_out[:FOX_WIDTH]), (mixed, w_out[FOX_WIDTH:])]
        else:
            o = layer // 2
            p = dict(mu=rw_mu[o].astype(F32),
                     w_r=rw_w_r[o].astype(BF16), w_k=rw_w_k[o].astype(BF16), w_v=rw_w_v[o].astype(BF16),
                     w0=_row(rw_w0[o]), w1=_pad_to(rw_w1[o], 1, LANES).astype(BF16),
                     w2=_pad_to(rw_w2[o], 0, LANES).astype(BF16),
                     a0=_row(rw_a0[o]), a1=_pad_to(rw_a1[o], 1, LANES).astype(BF16),
                     a2=_pad_to(rw_a2[o], 0, LANES).astype(BF16),
                     g1=_pad_to(rw_g1[o], 1, 2 * LANES).astype(BF16),
                     g2=_pad_to(rw_g2[o], 0, 2 * LANES).astype(BF16),
                     k_k=_row(rw_k_k[o]), k_a=_row(rw_k_a[o]), r_k=_row(rw_r_k[o]),
                     ln_w=_row(rw_ln_w[o]), ln_b=_row(rw_ln_b[o]))
            if o > 0:
                p.update(v0=_row(rw_v0[o - 1]), v1=_pad_to(rw_v1[o - 1], 1, LANES).astype(BF16),
                         v2=_pad_to(rw_v2[o - 1], 0, LANES).astype(BF16))
            r, lw, k, v, a, g = _rwkv_proj(xs, gain, p, v_first if o > 0 else None, seq)
            if o == 0:
                v_first = v
            mixes = [(_rwkv_scan(r, lw, k, v, a, g, p, batch, seq), rw_w_o[o].astype(BF16))]
        xs = _proj_ffn(xs, mixes, _row(ffn_norm[layer]), ffn_w_gate[layer].astype(BF16),
                       ffn_w_up[layer].astype(BF16), ffn_w_down[layer].astype(BF16))
    return xs.reshape(batch, seq, d_model)
```

```python
import functools

import jax
import jax.numpy as jnp
from jax import lax
from jax.experimental import pallas as pl
from jax.experimental.pallas import tpu as pltpu

F32 = jnp.float32
BF16 = jnp.bfloat16

D_MODEL = 1024
HEAD_DIM = 64
LANES = 128
FOX_HEADS = 8
FOX_WIDTH = FOX_HEADS * HEAD_DIM
FOX_PAIRS = FOX_HEADS // 2
POOL_WINDOWS = (2, 4, 8, 16)
POOL_WIDTH = D_MODEL - FOX_WIDTH
POOL_HALO = 16
RMS_EPS = 1e-6
GN_EPS = 64e-5
CHUNK = 64
DECAY_SCALE = 0.6065306597126334
NEG_BIG = -1e30
LOG2E = 1.4426950408889634
ATTN_SUB = 512
ATTN_GROUP = 4
VT_ROWS = 80
VMEM_LIMIT = 56 * 1024 * 1024


def _cparams(*sem, flags=None):
    return pltpu.CompilerParams(dimension_semantics=sem, vmem_limit_bytes=VMEM_LIMIT, flags=flags)


def _rms(x, gain):
    ms = jnp.mean(x * x, axis=-1, keepdims=True)
    return x * lax.rsqrt(ms + RMS_EPS) * gain


def _dot(a, b):
    return jnp.dot(a, b, preferred_element_type=F32)


def _dot_nt(a, b):
    return lax.dot_general(a, b, (((1,), (1,)), ((), ())), preferred_element_type=F32)


def _dot_tn(a, b):
    return lax.dot_general(a, b, (((0,), (0,)), ((), ())), preferred_element_type=F32)


def _const_spec(shape):
    nd = len(shape)
    return pl.BlockSpec(shape, lambda *_: (0,) * nd, pipeline_mode=pl.Buffered(1))


def _pair_rms_scale(t, first):
    t2 = t * t
    s0 = jnp.sum(jnp.where(first, t2, 0.0), axis=-1, keepdims=True)
    s1 = jnp.sum(jnp.where(first, 0.0, t2), axis=-1, keepdims=True)
    return jnp.where(first, lax.rsqrt(s0 / HEAD_DIM + RMS_EPS), lax.rsqrt(s1 / HEAD_DIM + RMS_EPS))


def _split3_bf16(t):
    hi = t.astype(BF16)
    r = t - hi.astype(F32)
    mid = r.astype(BF16)
    return hi, mid, (r - mid.astype(F32)).astype(BF16)


def _inproj_even_kernel(x_ref, gain_ref, wk_ref, wqg_ref, wu_ref, wvt_ref, wf_ref, fb_ref, kg_ref, tri_ref,
                        pw_ref, ps_ref, qg_ref, kaug_ref, vt_ref, mixed_ref, carry_ref, halo_ref, *, tiles_per_seq):
    i = pl.program_id(0)
    hb = _rms(x_ref[...], gain_ref[...]).astype(BF16)
    z = _dot(hb, wf_ref[...]) + fb_ref[...]
    c = jnp.minimum(z, 0.0) - jnp.log1p(jnp.exp(-jnp.abs(z)))
    tm = c.shape[0]
    c3 = _dot(tri_ref[...], jnp.concatenate(_split3_bf16(c), axis=1))
    c = c3[:, :LANES] + c3[:, LANES:2 * LANES] + c3[:, 2 * LANES:]

    @pl.when(i % tiles_per_seq == 0)
    def _():
        carry_ref[...] = jnp.zeros_like(carry_ref)
        halo_ref[...] = jnp.zeros_like(halo_ref)

    c = c + carry_ref[...]
    carry_ref[...] = c[tm - 1:tm, :]
    main = _dot(hb, wk_ref[...])
    lane = lax.broadcasted_iota(jnp.int32, (1, LANES), 1)
    first = lane < HEAD_DIM
    key_bias = c * (-LOG2E)
    for p in range(FOX_PAIRS):
        k = main[:, p * LANES:(p + 1) * LANES]
        kn = k * _pair_rms_scale(k, first) * kg_ref[...]
        for h in range(2):
            hi, mid, lo = (t.astype(F32) for t in _split3_bf16(key_bias[:, 2 * p + h:2 * p + h + 1]))
            spare = HEAD_DIM * (1 - h)
            pieces = jnp.where(lane == spare, hi, jnp.where(lane == spare + 1, mid,
                                                            jnp.where(lane == spare + 2, lo, 0.0)))
            own = first if h == 0 else jnp.logical_not(first)
            kaug_ref[:, (2 * p + h) * LANES:(2 * p + h + 1) * LANES] = jnp.where(own, kn, pieces).astype(BF16)
    qg_ref[...] = _dot(hb, wqg_ref[...]).astype(BF16)
    u = _dot(hb, wu_ref[...])
    _pool_mix(u, halo_ref[...], i % tiles_per_seq, pw_ref, ps_ref, mixed_ref)
    halo_ref[...] = u[tm - POOL_HALO:, :]
    v_t = _dot_nt(wvt_ref[...], hb)
    ones_row = jnp.where(lax.broadcasted_iota(jnp.int32, (VT_ROWS - HEAD_DIM, tm), 0) == 0, 1.0, 0.0).astype(BF16)
    for h in range(FOX_HEADS):
        vt_ref[h * VT_ROWS:h * VT_ROWS + HEAD_DIM, :] = v_t[h * HEAD_DIM:(h + 1) * HEAD_DIM, :].astype(BF16)
        vt_ref[h * VT_ROWS + HEAD_DIM:(h + 1) * VT_ROWS, :] = ones_row


def _inproj_even(x, gain, w_k, w_qg, w_u, wv_t, wf, f_bias, k_gain2, pool_w, pool_scale, seq, tm=512):
    n = x.shape[0]
    tm = min(tm, seq)
    tri = jnp.tril(jnp.ones((tm, tm), BF16))
    consts = [gain, w_k, w_qg, w_u, wv_t, wf, f_bias, k_gain2, tri, pool_w, pool_scale]
    return pl.pallas_call(
        functools.partial(_inproj_even_kernel, tiles_per_seq=seq // tm),
        grid=(n // tm,),
        in_specs=[pl.BlockSpec((tm, D_MODEL), lambda i: (i, 0))] + [_const_spec(c.shape) for c in consts],
        out_specs=[pl.BlockSpec((tm, 2 * FOX_WIDTH), lambda i: (i, 0)),
                   pl.BlockSpec((tm, 2 * FOX_WIDTH), lambda i: (i, 0)),
                   pl.BlockSpec((FOX_HEADS * VT_ROWS, tm), lambda i: (0, i)),
                   pl.BlockSpec((tm, POOL_WIDTH), lambda i: (i, 0))],
        out_shape=[jax.ShapeDtypeStruct((n, 2 * FOX_WIDTH), BF16),
                   jax.ShapeDtypeStruct((n, 2 * FOX_WIDTH), BF16),
                   jax.ShapeDtypeStruct((FOX_HEADS * VT_ROWS, n), BF16),
                   jax.ShapeDtypeStruct((n, POOL_WIDTH), BF16)],
        scratch_shapes=[pltpu.VMEM((1, LANES), F32), pltpu.VMEM((POOL_HALO, POOL_WIDTH), F32)],
        compiler_params=_cparams("arbitrary"),
    )(x, *consts)


def _fox_attn_kernel(q_ref, og_ref, k0_ref, k1_ref, vt_ref, qg_ref, o_ref, *, tq):
    qi = pl.program_id(2)
    lane = lax.broadcasted_iota(jnp.int32, (1, LANES), 1)
    first = lane < HEAD_DIM
    k_refs = (k0_ref, k1_ref)

    q = q_ref[...].astype(F32)
    qn = q * _pair_rms_scale(q, first) * (qg_ref[...] * (HEAD_DIM ** -0.5 * LOG2E))
    ones0 = jnp.where((lane >= HEAD_DIM) & (lane < HEAD_DIM + 3), 1.0, 0.0)
    ones1 = jnp.where(lane < 3, 1.0, 0.0)
    q_heads = (jnp.where(first, qn, ones0).astype(BF16), jnp.where(first, ones1, qn).astype(BF16))
    n_sub = tq // ATTN_SUB
    chains = [(u, h) for u in range(n_sub) for h in range(2)]
    q_sub = [q_heads[h][u * ATTN_SUB:(u + 1) * ATTN_SUB, :] for u, h in chains]
    key_row = lax.broadcasted_iota(jnp.int32, (tq, ATTN_SUB), 0)
    q_col = lax.broadcasted_iota(jnp.int32, (tq, ATTN_SUB), 1)

    def step(blocks, stats):
        rows = [pl.ds(pl.multiple_of(j * tq, tq), tq) for j, _ in blocks]
        s = []
        for (_, diagonal), r in zip(blocks, rows):
            sb = [_dot_nt(k_refs[h][r, :], q_sub[i]) for i, (u, h) in enumerate(chains)]
            if diagonal:
                sb = [jnp.where(key_row <= q_col + u * ATTN_SUB, x, NEG_BIG) for x, (u, h) in zip(sb, chains)]
            s.append(sb)
        m_new = [functools.reduce(jnp.maximum, [jnp.max(sb[i], axis=0, keepdims=True) for sb in s], st[0])
                 for i, st in enumerate(stats)]
        alpha = [jnp.exp2(st[0] - m) for st, m in zip(stats, m_new)]
        p = [[jnp.exp2(x - m) for x, m in zip(sb, m_new)] for sb in s]
        pv = [sum(_dot(vt_ref[h * VT_ROWS:(h + 1) * VT_ROWS, r], pb[i].astype(BF16)) for pb, r in zip(p, rows))
              for i, (u, h) in enumerate(chains)]
        return tuple((m, a * st[1] + y) for m, a, st, y in zip(m_new, alpha, stats, pv))

    def finish(stats):
        for u in range(n_sub):
            (_, a0), (_, a1) = stats[2 * u], stats[2 * u + 1]
            attn = jnp.concatenate([a0[:HEAD_DIM] / a0[HEAD_DIM:HEAD_DIM + 1],
                                    a1[:HEAD_DIM] / a1[HEAD_DIM:HEAD_DIM + 1]], axis=0).T
            rows = slice(u * ATTN_SUB, (u + 1) * ATTN_SUB)
            o_ref[rows, :] = (attn * jax.nn.sigmoid(og_ref[rows, :].astype(F32))).astype(BF16)

    init = tuple((jnp.full((1, ATTN_SUB), NEG_BIG, F32), jnp.zeros((VT_ROWS, ATTN_SUB), F32)) for _ in chains)
    stats = lax.fori_loop(0, qi // ATTN_GROUP,
                          lambda i, c: step([(ATTN_GROUP * i + g, False) for g in range(ATTN_GROUP)], c), init)
    for left in range(ATTN_GROUP):
        @pl.when(qi % ATTN_GROUP == left)
        def _(left=left):
            finish(step([(qi - left + g, False) for g in range(left)] + [(qi, True)], stats))


def _fox_attn(qg, kaug, vt, q_gain2, batch, seq, tq=512):
    n = qg.shape[0]
    tq = min(tq, seq)
    nq = seq // tq
    return pl.pallas_call(
        functools.partial(_fox_attn_kernel, tq=tq),
        grid=(batch, FOX_PAIRS, nq),
        in_specs=[pl.BlockSpec((tq, LANES), lambda b, p, i: (b * nq + i, p)),
                  pl.BlockSpec((tq, LANES), lambda b, p, i: (b * nq + i, FOX_PAIRS + p)),
                  pl.BlockSpec((seq, LANES), lambda b, p, i: (b, 2 * p)),
                  pl.BlockSpec((seq, LANES), lambda b, p, i: (b, 2 * p + 1)),
                  pl.BlockSpec((2 * VT_ROWS, seq), lambda b, p, i: (p, b)),
                  _const_spec(q_gain2.shape)],
        out_specs=pl.BlockSpec((tq, LANES), lambda b, p, i: (b * nq + i, p)),
        out_shape=jax.ShapeDtypeStruct((n, FOX_WIDTH), BF16),
        compiler_params=_cparams("parallel", "parallel", "arbitrary"),
    )(qg, qg, kaug, kaug, vt, q_gain2)


def _split_bf16(t):
    hi = t.astype(BF16)
    lo = (t - hi.astype(F32)).astype(BF16)
    return hi, lo


def _pool_mix(u, halo, tile_in_seq, pw_ref, ps_ref, o_ref):
    tm = u.shape[0]
    row = lax.broadcasted_iota(jnp.int32, (tm, tm), 0)
    col = lax.broadcasted_iota(jnp.int32, (tm, tm), 1)
    hrow = lax.broadcasted_iota(jnp.int32, (tm, POOL_HALO), 0)
    hcol = lax.broadcasted_iota(jnp.int32, (tm, POOL_HALO), 1)
    pos = tile_in_seq * tm + lax.broadcasted_iota(jnp.int32, (tm, 1), 0)
    groups = range(len(POOL_WINDOWS))
    sl = [slice(g * LANES, (g + 1) * LANES) for g in groups]
    band = [jnp.where((col <= row) & (row - col < w), 1.0, 0.0).astype(BF16) for w in POOL_WINDOWS]
    hband = [jnp.where(hrow + POOL_HALO - hcol < w, 1.0, 0.0).astype(BF16) for w in POOL_WINDOWS]
    wsum2 = [_dot(band[g], jnp.concatenate(_split_bf16(u[:, sl[g]]), axis=1))
             + _dot(hband[g], jnp.concatenate(_split_bf16(halo[:, sl[g]]), axis=1)) for g in groups]
    pooled = [(wsum2[g][:, :LANES] + wsum2[g][:, LANES:]) / jnp.minimum(pos + 1, w).astype(F32) - u[:, sl[g]]
              for g, w in enumerate(POOL_WINDOWS)]
    mixed = [_dot(pooled[g].astype(BF16), pw_ref[g]) for g in groups]
    for g in groups:
        o_ref[:, sl[g]] = (mixed[g] * ps_ref[:, sl[g]]).astype(BF16)


def _proj_ffn_kernel(*refs, n_mix):
    x_ref = refs[0]
    mix_refs = refs[1:1 + 2 * n_mix]
    gain_ref, wg_ref, wu_ref, wd_ref, o_ref = refs[1 + 2 * n_mix:]
    x = x_ref[...]
    for m in range(n_mix):
        x = x + _dot(mix_refs[2 * m][...], mix_refs[2 * m + 1][...])
    hb = _rms(x, gain_ref[...]).astype(BF16)
    gate = _dot(hb, wg_ref[...])
    up = _dot(hb, wu_ref[...])
    act = (gate * jax.nn.sigmoid(gate) * up).astype(BF16)
    o_ref[...] = x + _dot(act, wd_ref[...])


def _proj_ffn(x, mixes, gain, w_gate, w_up, w_down, tm=512):
    n = x.shape[0]
    tm = min(tm, n)
    args, specs = [x], [pl.BlockSpec((tm, D_MODEL), lambda i: (i, 0))]
    for y, w in mixes:
        args += [y, w]
        specs += [pl.BlockSpec((tm, y.shape[1]), lambda i: (i, 0)), _const_spec(w.shape)]
    consts = [gain, w_gate, w_up, w_down]
    return pl.pallas_call(
        functools.partial(_proj_ffn_kernel, n_mix=len(mixes)),
        grid=(n // tm,),
        in_specs=specs + [_const_spec(c.shape) for c in consts],
        out_specs=pl.BlockSpec((tm, D_MODEL), lambda i: (i, 0)),
        out_shape=jax.ShapeDtypeStruct((n, D_MODEL), F32),
        compiler_params=_cparams("parallel"),
    )(*args, *consts)


def _rwkv_proj_kernel(*refs, tiles_per_seq, has_vmix):
    if has_vmix:
        (x_ref, halo_ref, gain_ref, mu_ref, wr_ref, wk_ref, wv_ref, w0_ref, w1_ref, w2_ref,
         a0_ref, a1_ref, a2_ref, g1_ref, g2_ref, vf_ref, v0_ref, v1_ref, v2_ref,
         r_ref, lw_ref, k_ref, v_ref, a_ref, g_ref) = refs
    else:
        (x_ref, halo_ref, gain_ref, mu_ref, wr_ref, wk_ref, wv_ref, w0_ref, w1_ref, w2_ref,
         a0_ref, a1_ref, a2_ref, g1_ref, g2_ref,
         r_ref, lw_ref, k_ref, v_ref, a_ref, g_ref) = refs
    i = pl.program_id(0)
    tm = x_ref.shape[0]
    gain = gain_ref[...]
    h = _rms(x_ref[...], gain)
    h_last = _rms(halo_ref[...], gain)[halo_ref.shape[0] - 1:, :]
    h_last = h_last * (i % tiles_per_seq != 0).astype(F32)
    rows = lax.broadcasted_iota(jnp.int32, (tm, 1), 0)
    h_prev = jnp.where(rows == 0, h_last, pltpu.roll(h, 1, axis=0))
    xx = h_prev - h

    def mix(idx):
        return (h + xx * mu_ref[idx:idx + 1, :]).astype(BF16)

    xr, xw, xk, xv, xa, xg = (mix(idx) for idx in range(6))
    r_ref[...] = _dot(xr, wr_ref[...]).astype(BF16)
    k_ref[...] = _dot(xk, wk_ref[...]).astype(BF16)
    v = _dot(xv, wv_ref[...])
    wl = w0_ref[...] + _dot(jnp.tanh(_dot(xw, w1_ref[...])).astype(BF16), w2_ref[...])
    lw_ref[...] = -DECAY_SCALE * jax.nn.sigmoid(wl)
    a_ref[...] = jax.nn.sigmoid(
        a0_ref[...] + _dot(_dot(xa, a1_ref[...]).astype(BF16), a2_ref[...])).astype(BF16)
    g_ref[...] = _dot(jax.nn.sigmoid(_dot(xg, g1_ref[...])).astype(BF16), g2_ref[...]).astype(BF16)
    if has_vmix:
        gate = jax.nn.sigmoid(v0_ref[...] + _dot(_dot(xv, v1_ref[...]).astype(BF16), v2_ref[...]))
        v = v + (vf_ref[...].astype(F32) - v) * gate
    v_ref[...] = v.astype(BF16)


def _rwkv_proj(x, gain, p, v_first, seq, tm=512):
    n = x.shape[0]
    tm = min(tm, seq)
    has_vmix = v_first is not None
    tile = pl.BlockSpec((tm, D_MODEL), lambda i: (i, 0))
    halo = pl.BlockSpec((8, D_MODEL), lambda i: (jnp.maximum(i * (tm // 8) - 1, 0), 0))
    consts = [gain, p['mu'], p['w_r'], p['w_k'], p['w_v'], p['w0'], p['w1'], p['w2'],
              p['a0'], p['a1'], p['a2'], p['g1'], p['g2']]
    args = [x, x] + consts
    specs = [tile, halo] + [_const_spec(c.shape) for c in consts]
    if has_vmix:
        extra = [p['v0'], p['v1'], p['v2']]
        args += [v_first] + extra
        specs += [tile] + [_const_spec(c.shape) for c in extra]
    out_dtypes = [BF16, F32, BF16, BF16, BF16, BF16]
    return pl.pallas_call(
        functools.partial(_rwkv_proj_kernel, tiles_per_seq=seq // tm, has_vmix=has_vmix),
        grid=(n // tm,),
        in_specs=specs,
        out_specs=[tile] * 6,
        out_shape=[jax.ShapeDtypeStruct((n, D_MODEL), dt) for dt in out_dtypes],
        compiler_params=_cparams("parallel"),
    )(*args)


def _rwkv_scan_kernel(r_ref, lw_ref, k_ref, v_ref, a_ref, g_ref, kk_ref, ka_ref, rk_ref, lnw_ref, lnb_ref,
                      o_ref, state_ref):
    t = pl.program_id(2)

    @pl.when(t == 0)
    def _():
        state_ref[...] = jnp.zeros_like(state_ref)

    L = CHUNK
    n_chunks = r_ref.shape[0] // L
    n_pairs = r_ref.shape[1] // LANES
    first = lax.broadcasted_iota(jnp.int32, (1, LANES), 1) < HEAD_DIM
    trow = lax.broadcasted_iota(jnp.int32, (L, LANES), 0)
    tcol = lax.broadcasted_iota(jnp.int32, (L, LANES), 1) & (HEAD_DIM - 1)
    strict = tcol < trow
    incl = tcol <= trow
    same16 = (trow >> 4) == (tcol >> 4)
    same32 = (trow >> 5) == (tcol >> 5)
    eye = jnp.where(trow == tcol, 1.0, 0.0)
    same_head = ((lax.broadcasted_iota(jnp.int32, (LANES, LANES), 0) >> 6)
                 == (lax.broadcasted_iota(jnp.int32, (LANES, LANES), 1) >> 6))
    tri = jnp.where(lax.broadcasted_iota(jnp.int32, (L, L), 1) <= lax.broadcasted_iota(jnp.int32, (L, L), 0),
                    1.0, 0.0).astype(BF16)

    def bf(z):
        return z.astype(BF16)

    def stack(zb):
        return jnp.concatenate([jnp.where(first, zb, 0), jnp.where(first, 0, zb)], axis=0)

    def wmm(x, y):
        return _dot(bf(x), stack(bf(y)))

    def head_sums(z):
        s0 = jnp.sum(jnp.where(first, z, 0.0), axis=-1, keepdims=True)
        s1 = jnp.sum(jnp.where(first, 0.0, z), axis=-1, keepdims=True)
        return jnp.where(first, s0, s1)

    def tile(ref, c, p):
        return ref[c * L:(c + 1) * L, p * LANES:(p + 1) * LANES]

    def process(items, states):
        def load(ref):
            return [tile(ref, c, p).astype(F32) for c, p in items]

        def lane_param(ref):
            return [ref[:, p * LANES:(p + 1) * LANES] for _, p in items]

        r, lw, kraw, v, asig = load(r_ref), load(lw_ref), load(k_ref), load(v_ref), load(a_ref)
        kk = [x * w for x, w in zip(kraw, lane_param(kk_ref))]
        kk = [x / jnp.maximum(jnp.sqrt(head_sums(x * x)), 1e-12) for x in kk]
        k = [x * (1.0 + (s - 1.0) * w) for x, s, w in zip(kraw, asig, lane_param(ka_ref))]
        b = [x * s for x, s in zip(kk, asig)]
        lw_hi = [bf(x) for x in lw]
        lw_lo = [bf(x - h.astype(F32)) for x, h in zip(lw, lw_hi)]
        cum = [_dot(tri, jnp.concatenate([h, l], axis=1)) for h, l in zip(lw_hi, lw_lo)]
        cum = [x[:, :LANES] + x[:, LANES:] for x in cum]
        cum_end = [x[L - 1:L, :] for x in cum]
        e_neg = [jnp.exp(-x) for x in cum]
        e_end = [jnp.exp(ce - x) for x, ce in zip(cum, cum_end)]
        rt = [x * jnp.exp(c) for x, c in zip(r, cum)]
        at_b = [bf(-x * jnp.exp(c - w)) for x, c, w in zip(kk, cum, lw)]
        bt = [bf(x * e) for x, e in zip(b, e_neg)]
        kt = [bf(x * e) for x, e in zip(k, e_neg)]
        bh = [bf(x * e) for x, e in zip(b, e_end)]
        kh = [bf(x * e) for x, e in zip(k, e_end)]
        v_b = [bf(x) for x in v]
        bonus = [head_sums(x * y * w) * z for x, y, w, z in zip(r, k, lane_param(rk_ref), v)]
        aa = [_dot_nt(jnp.concatenate([xa, bf(xr)], axis=0), jnp.concatenate([stack(xb), stack(xk)], axis=0))
              for xa, xr, xb, xk in zip(at_b, rt, bt, kt)]
        a_ab = [jnp.where(strict, x[:L, :LANES], 0.0) for x in aa]
        a_ak = [bf(jnp.where(strict, x[:L, LANES:], 0.0)) for x in aa]
        a_rb = [bf(jnp.where(incl, x[L:, :LANES], 0.0)) for x in aa]
        a_rk = [bf(jnp.where(incl, x[L:, LANES:], 0.0)) for x in aa]
        pw = [jnp.where(same16, x, 0.0) for x in a_ab]
        tinv = [eye + x for x in pw]
        pw = [wmm(x, x) for x in pw]
        for _ in range(2):
            both = [_dot(jnp.concatenate([bf(x), bf(p)], axis=0), stack(bf(p))) for x, p in zip(tinv, pw)]
            tinv = [x + y[:L] for x, y in zip(tinv, both)]
            pw = [y[L:] for y in both]
        tinv = [x + wmm(x, p) for x, p in zip(tinv, pw)]
        off = [jnp.where(same32 & ~same16, x, 0.0) for x in a_ab]
        tinv = [x + wmm(wmm(x, o), x) for x, o in zip(tinv, off)]
        off = [jnp.where(same32, 0.0, x) for x in a_ab]
        tinv = [x + wmm(wmm(x, o), x) for x, o in zip(tinv, off)]
        akv = [_dot(x, stack(y)) for x, y in zip(a_ak, v_b)]
        x12 = [_dot(bf(x), jnp.concatenate([stack(y), stack(bf(z))], axis=1))
               for x, y, z in zip(tinv, at_b, akv)]
        x1 = [bf(x[:, :LANES]) for x in x12]
        x2 = [bf(x[:, LANES:]) for x in x12]
        ab12 = [_dot(x, jnp.concatenate([stack(y), stack(z)], axis=1)) for x, y, z in zip(a_rb, x1, x2)]
        y1 = [bf(x + y[:, :LANES]) for x, y in zip(rt, ab12)]
        y2 = [y[:, LANES:] + _dot(x, stack(z)) for y, x, z in zip(ab12, a_rk, v_b)]
        m_low = [bf(jnp.where(same_head, _dot_tn(x, y), 0.0)) for x, y in zip(x1, bh)]
        c_full = [_dot_tn(jnp.concatenate([x, y], axis=0), jnp.concatenate([z, w], axis=0))
                  for x, y, z, w in zip(x2, v_b, bh, kh)]
        c_wide = [jnp.where(first, x[:L], x[L:]) for x in c_full]
        p_end = [jnp.exp(x) for x in cum_end]

        y = []
        for i, (c, p) in enumerate(items):
            state_b = bf(states[p])
            y.append(_dot_nt(y1[i], stack(state_b)) + y2[i])
            states[p] = states[p] * p_end[i] + _dot(state_b, m_low[i]) + c_wide[i]
        lnw, lnb = lane_param(lnw_ref), lane_param(lnb_ref)
        for i, (c, p) in enumerate(items):
            mean = head_sums(y[i]) / HEAD_DIM
            dev = y[i] - mean
            var = head_sums(dev * dev) / HEAD_DIM
            z = dev * lax.rsqrt(var + GN_EPS) * lnw[i] + lnb[i]
            o_ref[c * L:(c + 1) * L, p * LANES:(p + 1) * LANES] = (
                (z + bonus[i]) * tile(g_ref, c, p).astype(F32)).astype(BF16)
        return states

    states = [state_ref[:, p * LANES:(p + 1) * LANES] for p in range(n_pairs)]
    states = process([(c, p) for c in range(n_chunks) for p in range(n_pairs)], states)
    for p in range(n_pairs):
        state_ref[:, p * LANES:(p + 1) * LANES] = states[p]


def _rwkv_scan(r, lw, k, v, a, g, p, batch, seq, tb=512, lanes=8 * LANES):
    n = r.shape[0]
    tb = min(tb, seq)
    nt = seq // tb
    tile = pl.BlockSpec((tb, lanes), lambda b, h, t: (b * nt + t, h))
    lane_const = pl.BlockSpec((1, lanes), lambda b, h, t: (0, h))
    return pl.pallas_call(
        _rwkv_scan_kernel,
        grid=(batch, D_MODEL // lanes, nt),
        in_specs=[tile] * 6 + [lane_const] * 5,
        out_specs=tile,
        out_shape=jax.ShapeDtypeStruct((n, D_MODEL), BF16),
        scratch_shapes=[pltpu.VMEM((HEAD_DIM, lanes), F32)],
        compiler_params=_cparams("parallel", "parallel", "arbitrary"),
    )(r, lw, k, v, a, g, p['k_k'], p['k_a'], p['r_k'], p['ln_w'], p['ln_b'])


def _pad_to(t, axis, size):
    pad = [(0, 0)] * t.ndim
    pad[axis] = (0, size - t.shape[axis])
    return jnp.pad(t, pad)


def _row(t):
    return t.reshape(1, -1).astype(F32)


def _pair_tile(gain):
    return jnp.tile(gain.astype(F32), 2).reshape(1, LANES)


def kernel(x, mix_norm, ffn_norm, ffn_w_gate, ffn_w_up, ffn_w_down, hy_w_in, hy_f_bias, hy_q_gain, hy_k_gain, hy_pool_w, hy_pool_scale, hy_w_out, rw_mu, rw_w_r, rw_w_k, rw_w_v, rw_w0, rw_w1, rw_w2, rw_a0, rw_a1, rw_a2, rw_g1, rw_g2, rw_k_k, rw_k_a, rw_r_k, rw_ln_w, rw_ln_b, rw_w_o, rw_v0, rw_v1, rw_v2):
    batch, seq, d_model = x.shape
    assert d_model == D_MODEL and seq % 256 == 0
    depth = mix_norm.shape[0]
    n = batch * seq
    xs = x.reshape(n, d_model)
    v_first = None
    for layer in range(depth):
        gain = _row(mix_norm[layer])
        if layer % 2 == 0:
            e = layer // 2
            w_in = hy_w_in[e]
            w_qg = jnp.concatenate([w_in[:, :FOX_WIDTH], w_in[:, 3 * FOX_WIDTH:4 * FOX_WIDTH]], axis=1).astype(BF16)
            w_k = w_in[:, FOX_WIDTH:2 * FOX_WIDTH].astype(BF16)
            w_u = w_in[:, 4 * FOX_WIDTH + FOX_HEADS:].astype(BF16)
            wv_t = w_in[:, 2 * FOX_WIDTH:3 * FOX_WIDTH].T.astype(BF16)
            wf = _pad_to(w_in[:, 4 * FOX_WIDTH:4 * FOX_WIDTH + FOX_HEADS], 1, LANES).astype(BF16)
            f_bias = _pad_to(_row(hy_f_bias[e]), 1, LANES)
            qg, kaug, vt, mixed = _inproj_even(xs, gain, w_k, w_qg, w_u, wv_t, wf, f_bias,
                                               _pair_tile(hy_k_gain[e]), hy_pool_w[e].astype(BF16),
                                               _row(hy_pool_scale[e]), seq)
            attn = _fox_attn(qg, kaug, vt, _pair_tile(hy_q_gain[e]), batch, seq)
            w_out = hy_w_out[e].astype(BF16)
            mixes = [(attn, w_out[:FOX_WIDTH]), (mixed, w_out[FOX_WIDTH:])]
        else:
            o = layer // 2
            p = dict(mu=rw_mu[o].astype(F32),
                     w_r=rw_w_r[o].astype(BF16), w_k=rw_w_k[o].astype(BF16), w_v=rw_w_v[o].astype(BF16),
                     w0=_row(rw_w0[o]), w1=_pad_to(rw_w1[o], 1, LANES).astype(BF16),
                     w2=_pad_to(rw_w2[o], 0, LANES).astype(BF16),
                     a0=_row(rw_a0[o]), a1=_pad_to(rw_a1[o], 1, LANES).astype(BF16),
                     a2=_pad_to(rw_a2[o], 0, LANES).astype(BF16),
                     g1=_pad_to(rw_g1[o], 1, 2 * LANES).astype(BF16),
                     g2=_pad_to(rw_g2[o], 0, 2 * LANES).astype(BF16),
                     k_k=_row(rw_k_k[o]), k_a=_row(rw_k_a[o]), r_k=_row(rw_r_k[o]),
                     ln_w=_row(rw_ln_w[o]), ln_b=_row(rw_ln_b[o]))
            if o > 0:
                p.update(v0=_row(rw_v0[o - 1]), v1=_pad_to(rw_v1[o - 1], 1, LANES).astype(BF16),
                         v2=_pad_to(rw_v2[o - 1], 0, LANES).astype(BF16))
            r, lw, k, v, a, g = _rwkv_proj(xs, gain, p, v_first if o > 0 else None, seq)
            if o == 0:
                v_first = v
            mixes = [(_rwkv_scan(r, lw, k, v, a, g, p, batch, seq), rw_w_o[o].astype(BF16))]
        xs = _proj_ffn(xs, mixes, _row(ffn_norm[layer]), ffn_w_gate[layer].astype(BF16),
                       ffn_w_up[layer].astype(BF16), ffn_w_down[layer].astype(BF16))
    return xs.reshape(batch, seq, d_model)
```
